```python
import jax, jax.numpy as jnp
from jax import lax
import numpy as np

D_MODEL = 2048
BATCH = 1
SEQ = 8192
DEPTH = 2
DEC_BATCH = 32
DEC_SEQ = 1
PAST_LEN = 8192
PAGE_SIZE = 128

BRANCH_W = D_MODEL // 2
CHUNK = 128
A_GROUP_W = 128
A_GROUPS = BRANCH_W // A_GROUP_W
SB_HEAD_DIM = 128
SB_HEADS = BRANCH_W // SB_HEAD_DIM
SB_BLOCK = 128
SB_BIAS_INIT = -6.0
GLA_HEADS = 4
GLA_DV = BRANCH_W // GLA_HEADS
GLA_DK = GLA_DV // 2
GLA_KEY = GLA_HEADS * GLA_DK
GLA_RANK = 16
GLA_TAU = 16.0
GLA_CHUNK = 64
N_BRANCH = 3
D_FF = -(-(8 * D_MODEL) // (3 * 256)) * 256
EPS = 1e-6
IN_SIZES = (BRANCH_W, BRANCH_W, BRANCH_W, BRANCH_W, BRANCH_W, GLA_KEY, GLA_KEY, BRANCH_W, BRANCH_W, GLA_RANK, N_BRANCH * D_MODEL)
IN_SPLITS = tuple(int(s) for s in np.cumsum(IN_SIZES)[:-1])
D_IN = sum(IN_SIZES)

kernel_name = 'hybrid_sgu_stickbreak_gla_decoder'


def rmsnorm(x, g):
    xf = x.astype(jnp.float32)
    xf = xf * lax.rsqrt(jnp.mean(xf * xf, axis=-1, keepdims=True) + EPS)
    return (xf * g.astype(jnp.float32)).astype(x.dtype)


def layernorm(x, g):
    xf = x.astype(jnp.float32)
    xf = xf - jnp.mean(xf, axis=-1, keepdims=True)
    xf = xf * lax.rsqrt(jnp.mean(xf * xf, axis=-1, keepdims=True) + EPS)
    return (xf * g.astype(jnp.float32)).astype(x.dtype)


def chunk_spatial_mix(v, w_s, b_s):
    B, L, _ = v.shape
    n = -(-L // CHUNK)
    vp = jnp.pad(v, ((0, 0), (0, n * CHUNK - L), (0, 0))).reshape(B, n, CHUNK, A_GROUPS, A_GROUP_W)
    s = jnp.einsum('gts,bnsgc->bntgc', jnp.tril(w_s), vp) + b_s.T[:, :, None]
    return s.reshape(B, n * CHUNK, BRANCH_W)[:, :L]


def stick_breaking(q, k, v, q_pos, k_pos, bias):
    z = jnp.einsum('bqhd,bkhd->bhqk', q, k).astype(jnp.float32) * (SB_HEAD_DIM ** -0.5)
    z = z + bias.astype(jnp.float32)[None, :, None, None]
    mask = (k_pos[None, :] < q_pos[:, None])[None, None]
    log_beta = jax.nn.log_sigmoid(z)
    log_rest = jnp.where(mask, jax.nn.log_sigmoid(-z), 0.0)
    tail = lax.cumsum(log_rest, axis=3, reverse=True) - log_rest
    a = jnp.where(mask, jnp.exp(log_beta + tail), 0.0)
    return jnp.einsum('bhqk,bkhd->bqhd', a.astype(v.dtype), v)


def stick_breaking_prompt(q, k, v, bias):
    B, L, H, D = q.shape
    nb = L // SB_BLOCK
    pos = jnp.arange(L)
    qb = q.reshape(B, nb, SB_BLOCK, H, D).swapaxes(0, 1)
    pb = pos.reshape(nb, SB_BLOCK)
    o = lax.map(lambda a: stick_breaking(a[0], k, v, a[1], pos, bias), (qb, pb))
    return o.swapaxes(0, 1).reshape(B, L, H, D)


def gla_chunked(q, k, v, log_a, s0):
    B, L, H, _ = q.shape
    c = min(GLA_CHUNK, L)
    n = -(-L // c)

    def blocks(t):
        t = jnp.pad(t.astype(jnp.float32), ((0, 0), (0, n * c - L), (0, 0), (0, 0)))
        return jnp.moveaxis(t.reshape(B, n, c, H, t.shape[-1]), 1, 0)

    tri = jnp.tril(jnp.ones((c, c), dtype=bool))[None, :, :, None, None]

    def step(S, inp):
        qc, kc, vc, ac = inp
        b = jnp.cumsum(ac, axis=1)
        decay = jnp.exp(jnp.where(tri, b[:, :, None] - b[:, None, :], -jnp.inf))
        scores = jnp.einsum('bthd,bshd,btshd->bhts', qc, kc, decay)
        o = jnp.einsum('bhts,bshv->bthv', scores, vc) + jnp.einsum('bthd,bhdv->bthv', qc * jnp.exp(b), S)
        b_end = b[:, -1]
        S = jnp.exp(b_end)[..., None] * S + jnp.einsum('bshd,bshv->bhdv', kc * jnp.exp(b_end[:, None] - b), vc)
        return S, o

    S, o = lax.scan(step, s0.astype(jnp.float32), (blocks(q * GLA_DK ** -0.5), blocks(k), blocks(v), blocks(log_a)))
    o = jnp.moveaxis(o, 0, 1).reshape(B, n * c, H, v.shape[-1])[:, :L]
    return o, S


def run_group(x, w, cache_k, cache_v, page_table, state_gla):
    sample = cache_k is not None
    B, L, _ = x.shape
    new_k, new_v, new_s, new_cv = [], [], [], []
    for l in range(DEPTH):
        h = rmsnorm(x, w['g_mix'][l])
        p = h @ w['w_in'][l]
        a_u, a_v, sb_q, sb_k, sb_v, gl_q, gl_k, gl_v, gl_r, gl_a, gate_in = jnp.split(p, IN_SPLITS, axis=-1)
        sgu_v = layernorm(jax.nn.gelu(a_v, approximate=False), w['sgu_gain'][l])
        o_a = jax.nn.gelu(a_u, approximate=False) * chunk_spatial_mix(sgu_v, w['w_spatial'][l], w['b_spatial'][l])
        q = sb_q.reshape(B, L, SB_HEADS, SB_HEAD_DIM)
        k = sb_k.reshape(B, L, SB_HEADS, SB_HEAD_DIM)
        v = sb_v.reshape(B, L, SB_HEADS, SB_HEAD_DIM)
        if sample:
            past_k = cache_k[l][page_table].reshape(B, -1, SB_HEADS, SB_HEAD_DIM)
            past_v = cache_v[l][page_table].reshape(B, -1, SB_HEADS, SB_HEAD_DIM)
            past = past_k.shape[1]
            k_all = jnp.concatenate([past_k.astype(k.dtype), k], axis=1)
            v_all = jnp.concatenate([past_v.astype(v.dtype), v], axis=1)
            o_b = stick_breaking(q, k_all, v_all, past + jnp.arange(L), jnp.arange(past + L), w['sb_bias'][l])
            s0 = state_gla[l]
        else:
            o_b = stick_breaking_prompt(q, k, v, w['sb_bias'][l])
            s0 = jnp.zeros((B, GLA_HEADS, GLA_DK, GLA_DV), jnp.float32)
        o_b = o_b.reshape(B, L, BRANCH_W)
        log_a = jax.nn.log_sigmoid((gl_a @ w['w_gla_a2'][l] + w['b_gla_a'][l]).astype(jnp.float32)) / GLA_TAU
        o_c, s_new = gla_chunked(gl_q.reshape(B, L, GLA_HEADS, GLA_DK), gl_k.reshape(B, L, GLA_HEADS, GLA_DK),
                                 gl_v.reshape(B, L, GLA_HEADS, GLA_DV), log_a.reshape(B, L, GLA_HEADS, GLA_DK), s0)
        o_c = rmsnorm(o_c, w['g_gla_out'][l].reshape(GLA_HEADS, GLA_DV)).reshape(B, L, BRANCH_W).astype(x.dtype)
        o_c = o_c * jax.nn.silu(gl_r)
        branches = jnp.stack([o_a, o_b, o_c], axis=2)
        y = jnp.einsum('blnc,ncd->blnd', branches, w['w_branch'][l])
        g = jax.nn.sigmoid(gate_in.reshape(B, L, N_BRANCH, D_MODEL))
        x = x + jnp.sum(g * y, axis=2) @ w['w_o'][l]
        h2 = rmsnorm(x, w['g_ffn'][l])
        f_gate, f_up = jnp.split(h2 @ w['w_ffn_in'][l], 2, axis=-1)
        x = x + (jax.nn.silu(f_gate) * f_up) @ w['w_ffn_out'][l]
        new_k.append(k)
        new_v.append(v)
        new_s.append(s_new)
        if sample:
            new_cv.append(sgu_v)
    cv = jnp.stack(new_cv) if sample else None
    return rmsnorm(x, w['g_final']), jnp.stack(new_k), jnp.stack(new_v), jnp.stack(new_s), cv


def setup_inputs(seed: int = 0) -> dict:
    key = jax.random.key(seed)
    ks = jax.random.split(key, 24)
    n_pages = PAST_LEN // PAGE_SIZE
    n_used = DEC_BATCH * n_pages
    n_phys = n_used + n_used // 4
    f32 = jnp.float32

    def nrm(k, shape, scale):
        return jax.random.normal(k, shape, f32) * scale

    page_table = jax.random.permutation(ks[5], n_phys)[:n_used].reshape(DEC_BATCH, n_pages).astype(jnp.int32)
    return {
        'x_prompt': nrm(ks[0], (BATCH, SEQ, D_MODEL), 1.0),
        'x_sample': nrm(ks[1], (DEC_BATCH, DEC_SEQ, D_MODEL), 1.0),
        'cache_k': nrm(ks[2], (DEPTH, n_phys, PAGE_SIZE, SB_HEADS, SB_HEAD_DIM), 1.0),
        'cache_v': nrm(ks[3], (DEPTH, n_phys, PAGE_SIZE, SB_HEADS, SB_HEAD_DIM), 1.0),
        'state_gla': nrm(ks[4], (DEPTH, DEC_BATCH, GLA_HEADS, GLA_DK, GLA_DV), 1.0),
        'page_table': page_table,
        'g_mix': 1.0 + nrm(ks[6], (DEPTH, D_MODEL), 0.05),
        'w_in': nrm(ks[7], (DEPTH, D_MODEL, D_IN), D_MODEL ** -0.5),
        'sgu_gain': 1.0 + nrm(ks[8], (DEPTH, BRANCH_W), 0.05),
        'w_spatial': nrm(ks[9], (DEPTH, A_GROUPS, CHUNK, CHUNK), CHUNK ** -0.5),
        'b_spatial': 1.0 + nrm(ks[10], (DEPTH, A_GROUPS, CHUNK), 0.05),
        'sb_bias': SB_BIAS_INIT + nrm(ks[20], (DEPTH, SB_HEADS), 0.1),
        'w_gla_a2': nrm(ks[11], (DEPTH, GLA_RANK, GLA_KEY), GLA_RANK ** -0.5),
        'b_gla_a': nrm(ks[12], (DEPTH, GLA_KEY), 0.05),
        'g_gla_out': 1.0 + nrm(ks[13], (DEPTH, BRANCH_W), 0.05),
        'w_branch': nrm(ks[14], (DEPTH, N_BRANCH, BRANCH_W, D_MODEL), BRANCH_W ** -0.5),
        'w_o': nrm(ks[15], (DEPTH, D_MODEL, D_MODEL), D_MODEL ** -0.5),
        'g_ffn': 1.0 + nrm(ks[16], (DEPTH, D_MODEL), 0.05),
        'w_ffn_in': nrm(ks[17], (DEPTH, D_MODEL, 2 * D_FF), D_MODEL ** -0.5),
        'w_ffn_out': nrm(ks[18], (DEPTH, D_FF, D_MODEL), D_FF ** -0.5),
        'g_final': 1.0 + nrm(ks[19], (D_MODEL,), 0.05),
    }


def reference(x_prompt, x_sample, cache_k, cache_v, state_gla, page_table, g_mix, w_in, sgu_gain, w_spatial,
              b_spatial, sb_bias, w_gla_a2, b_gla_a, g_gla_out, w_branch, w_o, g_ffn, w_ffn_in, w_ffn_out, g_final):
    w = {'g_mix': g_mix, 'w_in': w_in, 'sgu_gain': sgu_gain, 'w_spatial': w_spatial, 'b_spatial': b_spatial,
         'sb_bias': sb_bias, 'w_gla_a2': w_gla_a2, 'b_gla_a': b_gla_a, 'g_gla_out': g_gla_out,
         'w_branch': w_branch, 'w_o': w_o, 'g_ffn': g_ffn, 'w_ffn_in': w_ffn_in, 'w_ffn_out': w_ffn_out,
         'g_final': g_final}
    y_prompt, k_prompt, v_prompt, gla_prompt, _ = run_group(x_prompt, w, None, None, None, None)
    y_sample, k_sample, v_sample, gla_sample, sgu_v_sample = run_group(x_sample, w, cache_k, cache_v, page_table, state_gla)
    return (y_prompt, y_sample, k_prompt, v_prompt, k_sample, v_sample, gla_prompt, gla_sample, sgu_v_sample)
```

```python
import functools

import jax
import jax.numpy as jnp
from jax import lax
from jax.experimental import pallas as pl
from jax.experimental.pallas import tpu as pltpu

f32 = jnp.float32
bf16 = jnp.bfloat16

D_MODEL = 2048
BRANCH_W = 1024
GROUP_W = 128
N_GROUPS = 8
SB_HEADS = 8
SB_DIM = 128
GLA_HEADS = 4
GLA_DK = 128
GLA_DV = 256
GLA_KEY = GLA_HEADS * GLA_DK
GLA_RANK = 16
GLA_TAU = 16.0
N_BRANCH = 3
D_FF = 5632
EPS = 1e-6
PAGE = 128

C_AU, C_AV, C_SQ, C_SK, C_SV = 0, 1024, 2048, 3072, 4096
C_GQ, C_GK, C_GV, C_GR = 5120, 5632, 6144, 7168
C_GATE = 8192
N_MAIN = C_GATE + N_BRANCH * D_MODEL
A_PAD = 128

V7X_VMEM_LIMIT = 52 * 1024 * 1024

SB_TILE = 256
GLA_BLOCK = 256
GLA_SUB = 16
DECODE_PAGES_PER_STEP = 8

NT_DIMS = (((1,), (1,)), ((), ()))
TN_DIMS = (((0,), (0,)), ((), ()))


def _params(*sem):
    return pltpu.CompilerParams(dimension_semantics=sem, vmem_limit_bytes=V7X_VMEM_LIMIT)


def _dot(a, b):
    return jnp.dot(a, b, preferred_element_type=f32)


def _gelu(x):
    return 0.5 * x * (1.0 + lax.erf(x * 0.7071067811865476))


def _sigmoid(x):
    return 1.0 / (1.0 + jnp.exp(-x))


def _silu(x):
    return x * _sigmoid(x)


def _softplus_neg_abs(z):
    return jnp.log1p(jnp.exp(-jnp.abs(z)))


def _split_bf16(x):
    hi = x.astype(bf16)
    lo = (x - hi.astype(f32)).astype(bf16)
    return hi, lo


def _rms(x, g):
    ms = jnp.mean(x * x, axis=-1, keepdims=True)
    return x * lax.rsqrt(ms + EPS) * g


def _norm_matmul_kernel(x_ref, g_ref, w_ref, o_ref, h_ref):
    @pl.when(pl.program_id(1) == 0)
    def _():
        h_ref[...] = _rms(x_ref[...], g_ref[...]).astype(bf16)

    o_ref[...] = _dot(h_ref[...], w_ref[...])


def _norm_matmul(x, g, w, tm, tn):
    m, d = x.shape
    n = w.shape[1]
    return pl.pallas_call(
        _norm_matmul_kernel,
        out_shape=jax.ShapeDtypeStruct((m, n), f32),
        grid=(m // tm, n // tn),
        in_specs=[pl.BlockSpec((tm, d), lambda i, j: (i, 0)),
                  pl.BlockSpec((1, d), lambda i, j: (0, 0)),
                  pl.BlockSpec((d, tn), lambda i, j: (0, j))],
        out_specs=pl.BlockSpec((tm, tn), lambda i, j: (i, j)),
        scratch_shapes=[pltpu.VMEM((tm, d), bf16)],
        compiler_params=_params("parallel", "arbitrary"),
        name="norm_matmul",
    )(x, g, w)


def _sgu_norm(av, gain):
    v = _gelu(av)
    v = v - jnp.mean(v, axis=-1, keepdims=True)
    v = v * lax.rsqrt(jnp.mean(v * v, axis=-1, keepdims=True) + EPS)
    return v * gain


def _sgu_prompt_kernel(au_ref, av_ref, gain_ref, ws_ref, bs_ref, o_ref, sv_ref):
    rows = au_ref.shape[0]
    sv_ref[...] = _sgu_norm(av_ref[...], gain_ref[...]).astype(bf16)
    for c in range(rows // GROUP_W):
        r = pl.ds(c * GROUP_W, GROUP_W)
        for g in range(N_GROUPS):
            cols = pl.ds(g * GROUP_W, GROUP_W)
            s = _dot(ws_ref[g], sv_ref[r, cols]) + bs_ref[:, cols]
            o_ref[r, cols] = (_gelu(au_ref[r, cols]) * s).astype(bf16)


def _sgu_prompt(p, gain, ws_tril, bs_full, rows):
    m = p.shape[0]
    return pl.pallas_call(
        _sgu_prompt_kernel,
        out_shape=jax.ShapeDtypeStruct((m, BRANCH_W), bf16),
        grid=(m // rows,),
        in_specs=[pl.BlockSpec((rows, BRANCH_W), lambda i: (i, C_AU // BRANCH_W)),
                  pl.BlockSpec((rows, BRANCH_W), lambda i: (i, C_AV // BRANCH_W)),
                  pl.BlockSpec((1, BRANCH_W), lambda i: (0, 0)),
                  pl.BlockSpec((N_GROUPS, GROUP_W, GROUP_W), lambda i: (0, 0, 0)),
                  pl.BlockSpec((GROUP_W, BRANCH_W), lambda i: (0, 0))],
        out_specs=pl.BlockSpec((rows, BRANCH_W), lambda i: (i, 0)),
        scratch_shapes=[pltpu.VMEM((rows, BRANCH_W), bf16)],
        compiler_params=_params("parallel"),
        name="sgu_prompt",
    )(p, p, gain, ws_tril, bs_full)


def _sgu_sample_kernel(au_ref, av_ref, gain_ref, w0_ref, b0_ref, o_ref, sv_ref):
    sv = _sgu_norm(av_ref[...], gain_ref[...])
    sv_ref[...] = sv
    o_ref[...] = (_gelu(au_ref[...]) * (w0_ref[...] * sv + b0_ref[...])).astype(bf16)


def _sgu_sample(p, gain, w0, b0):
    m = p.shape[0]
    row = pl.BlockSpec((1, BRANCH_W), lambda i: (0, 0))
    return pl.pallas_call(
        _sgu_sample_kernel,
        out_shape=(jax.ShapeDtypeStruct((m, BRANCH_W), bf16), jax.ShapeDtypeStruct((m, BRANCH_W), f32)),
        grid=(1,),
        in_specs=[pl.BlockSpec((m, BRANCH_W), lambda i: (0, C_AU // BRANCH_W)),
                  pl.BlockSpec((m, BRANCH_W), lambda i: (0, C_AV // BRANCH_W)),
                  row, row, row],
        out_specs=(pl.BlockSpec((m, BRANCH_W), lambda i: (0, 0)), pl.BlockSpec((m, BRANCH_W), lambda i: (0, 0))),
        compiler_params=_params("arbitrary"),
        name="sgu_sample",
    )(p, p, gain, w0, b0)


def _stick_tile(z, upper, carry, causal):
    sp = _softplus_neg_abs(z)
    log_beta = jnp.minimum(z, 0.0) - sp
    log_rest = jnp.minimum(-z, 0.0) - sp
    if causal is not None:
        log_rest = jnp.where(causal, log_rest, 0.0)
    hi, lo = _split_bf16(log_rest)
    tail = _dot(hi, upper) + _dot(lo, upper) + carry
    a = jnp.exp(log_beta + tail)
    if causal is not None:
        a = jnp.where(causal, a, 0.0)
    return a, carry + jnp.sum(log_rest, axis=1, keepdims=True)


def _strict_upper(t):
    r = lax.broadcasted_iota(jnp.int32, (t, t), 0)
    c = lax.broadcasted_iota(jnp.int32, (t, t), 1)
    return r, c


def _sb_prompt_kernel(bias_ref, q_ref, k_ref, v_ref, o_ref):
    t = q_ref.shape[0]
    h = pl.program_id(0)
    i = pl.program_id(1)
    bias = bias_ref[h]
    scale = SB_DIM ** -0.5
    q = q_ref[...].astype(bf16)
    r, c = _strict_upper(t)
    upper = (r > c).astype(bf16)

    def tile(j, carry, acc, causal):
        rows = pl.ds(pl.multiple_of(j * t, t), t)
        kb = k_ref[rows, :].astype(bf16)
        vb = v_ref[rows, :].astype(bf16)
        z = lax.dot_general(q, kb, NT_DIMS, preferred_element_type=f32) * scale + bias
        a, carry = _stick_tile(z, upper, carry, causal)
        return carry, acc + _dot(a.astype(bf16), vb)

    carry, acc = tile(i, jnp.zeros((t, 1), f32), jnp.zeros((t, SB_DIM), f32), c < r)

    def body(n, state):
        return tile(i - 1 - n, state[0], state[1], None)

    _, acc = lax.fori_loop(0, i, body, (carry, acc))
    o_ref[...] = acc.astype(bf16)


def _sb_prompt(p, bias):
    m = p.shape[0]
    t = min(SB_TILE, m)
    return pl.pallas_call(
        _sb_prompt_kernel,
        out_shape=jax.ShapeDtypeStruct((m, BRANCH_W), bf16),
        grid=(SB_HEADS, m // t),
        in_specs=[pl.BlockSpec(memory_space=pltpu.SMEM),
                  pl.BlockSpec((t, SB_DIM), lambda h, i: (i, C_SQ // SB_DIM + h)),
                  pl.BlockSpec((m, SB_DIM), lambda h, i: (0, C_SK // SB_DIM + h)),
                  pl.BlockSpec((m, SB_DIM), lambda h, i: (0, C_SV // SB_DIM + h))],
        out_specs=pl.BlockSpec((t, SB_DIM), lambda h, i: (i, h)),
        compiler_params=_params("parallel", "arbitrary"),
        name="sb_prompt",
    )(bias, p, p, p)


def _sb_decode_kernel(pt_ref, q_ref, bias_ref, *refs, pps):
    k_refs, v_refs = refs[:pps], refs[pps:2 * pps]
    o_ref, acc_ref, carry_ref = refs[2 * pps:]
    j = pl.program_id(1)

    @pl.when(j == 0)
    def _():
        acc_ref[...] = jnp.zeros_like(acc_ref)
        carry_ref[...] = jnp.zeros_like(carry_ref)

    scale = SB_DIM ** -0.5
    sub = lax.broadcasted_iota(jnp.int32, (SB_HEADS, BRANCH_W), 0)
    lane_head = lax.broadcasted_iota(jnp.int32, (SB_HEADS, BRANCH_W), 1) // SB_DIM
    own = sub == lane_head
    q_heads = jnp.where(own, jnp.broadcast_to(q_ref[...], (SB_HEADS, BRANCH_W)), 0.0).astype(bf16)
    r, c = _strict_upper(PAGE)
    upper = (r > c).astype(bf16)
    bias = bias_ref[...]

    def head_major(ref):
        return jnp.concatenate([ref[pl.ds(h, PAGE, stride=SB_HEADS), :] for h in range(SB_HEADS)],
                               axis=1).astype(bf16)

    acc = acc_ref[...]
    carry = carry_ref[:, 0:1]
    for u in range(pps):
        z = lax.dot_general(q_heads, head_major(k_refs[u]), NT_DIMS, preferred_element_type=f32) * scale + bias
        a, carry = _stick_tile(z, upper, carry, None)
        acc = acc + _dot(a.astype(bf16), head_major(v_refs[u]))
    acc_ref[...] = acc
    carry_ref[...] = jnp.broadcast_to(carry, carry_ref.shape)

    @pl.when(j == pl.num_programs(1) - 1)
    def _():
        o_ref[...] = jnp.sum(jnp.where(own, acc, 0.0), axis=0, keepdims=True)


def _sb_decode(q, bias_col, cache_k, cache_v, page_table, layer):
    b, n_pages = page_table.shape
    pps = DECODE_PAGES_PER_STEP
    while n_pages % pps:
        pps //= 2
    rows = PAGE * SB_HEADS

    def page_spec(u):
        return pl.BlockSpec((None, None, rows, SB_DIM),
                            lambda s, j, pt: (layer, pt[s, n_pages - 1 - (j * pps + u)], 0, 0))

    grid_spec = pltpu.PrefetchScalarGridSpec(
        num_scalar_prefetch=1,
        grid=(b, n_pages // pps),
        in_specs=[pl.BlockSpec((None, 1, BRANCH_W), lambda s, j, pt: (s, 0, 0)),
                  pl.BlockSpec((SB_HEADS, 1), lambda s, j, pt: (0, 0))]
                 + [page_spec(u) for u in range(pps)] * 2,
        out_specs=pl.BlockSpec((None, 1, BRANCH_W), lambda s, j, pt: (s, 0, 0)),
        scratch_shapes=[pltpu.VMEM((SB_HEADS, BRANCH_W), f32), pltpu.VMEM((SB_HEADS, PAGE), f32)],
    )
    return pl.pallas_call(
        functools.partial(_sb_decode_kernel, pps=pps),
        out_shape=jax.ShapeDtypeStruct((b, 1, BRANCH_W), f32),
        grid_spec=grid_spec,
        compiler_params=_params("parallel", "arbitrary"),
        name="sb_decode",
    )(page_table, q, bias_col, *([cache_k] * pps), *([cache_v] * pps))


def _gla_log_decay(ga, wa_ref, ba_ref):
    x = _dot(ga.astype(bf16), wa_ref[...]) + ba_ref[...]
    return (jnp.minimum(x, 0.0) - _softplus_neg_abs(x)) * (1.0 / GLA_TAU)


def _gla_out_norm(o, g, r):
    parts = []
    for h in range(GLA_HEADS):
        cols = slice(h * GLA_DV, (h + 1) * GLA_DV)
        parts.append(_rms(o[:, cols], g[:, cols]))
    return jnp.concatenate(parts, axis=1) * _silu(r)


def _gla_prompt_kernel(q_ref, k_ref, v_ref, r_ref, ga_ref, wa_ref, ba_ref, g_ref, o_ref, s_ref,
                       st_ref, b_ref, qe_ref, ke_ref, eb_ref, qs_ref, oacc_ref):
    t = q_ref.shape[0]
    sub = GLA_SUB
    step = pl.program_id(0)

    @pl.when(step == 0)
    def _():
        st_ref[...] = jnp.zeros_like(st_ref)

    log_a = _gla_log_decay(ga_ref[...], wa_ref, ba_ref)
    r = lax.broadcasted_iota(jnp.int32, (t, t), 0)
    c = lax.broadcasted_iota(jnp.int32, (t, t), 1)
    same = (r // sub) == (c // sub)
    incl = (same & (c <= r)).astype(bf16)
    whole = same.astype(bf16)
    hi, lo = _split_bf16(log_a)
    b = _dot(incl, hi) + _dot(incl, lo)
    b_end = _dot(whole, hi) + _dot(whole, lo)
    qs = q_ref[...] * (GLA_DK ** -0.5)
    b_ref[...] = b
    qs_ref[...] = qs
    qe_ref[...] = (qs * jnp.exp(b)).astype(bf16)
    ke_ref[...] = (k_ref[...] * jnp.exp(b_end - b)).astype(bf16)
    eb_ref[...] = jnp.exp(b_end)

    t_idx = lax.broadcasted_iota(jnp.int32, (sub, GLA_DK), 0)

    def body(n, _):
        rows = pl.ds(pl.multiple_of(n * sub, sub), sub)
        for h in range(GLA_HEADS):
            kc = pl.ds(h * GLA_DK, GLA_DK)
            vc = pl.ds(h * GLA_DV, GLA_DV)
            st = st_ref[h]
            o = lax.dot_general(qe_ref[rows, kc], st.astype(bf16), NT_DIMS, preferred_element_type=f32)
            bi = b_ref[rows, kc]
            qi = qs_ref[rows, kc]
            ki = k_ref[rows, kc]
            vi = v_ref[rows, vc]
            for s in range(sub):
                d = bi - bi[s:s + 1, :]
                e = jnp.exp(jnp.where(t_idx >= s, d, -jnp.inf))
                sc = jnp.sum(qi * e * ki[s:s + 1, :], axis=-1, keepdims=True)
                o = o + sc * vi[s:s + 1, :]
            oacc_ref[rows, vc] = o
            upd = lax.dot_general(vi.astype(bf16), ke_ref[rows, kc], TN_DIMS, preferred_element_type=f32)
            st_ref[h] = st * eb_ref[pl.ds(pl.multiple_of(n * sub, sub), 1), kc] + upd
        return 0

    lax.fori_loop(0, t // sub, body, 0)
    o_ref[...] = _gla_out_norm(oacc_ref[...], g_ref[...], r_ref[...]).astype(bf16)

    @pl.when(step == pl.num_programs(0) - 1)
    def _():
        for h in range(GLA_HEADS):
            s_ref[h] = st_ref[h].T


def _gla_prompt(p, pa, wa, ba, g):
    m = p.shape[0]
    t = min(GLA_BLOCK, m)
    return pl.pallas_call(
        _gla_prompt_kernel,
        out_shape=(jax.ShapeDtypeStruct((m, BRANCH_W), bf16),
                   jax.ShapeDtypeStruct((GLA_HEADS, GLA_DK, GLA_DV), f32)),
        grid=(m // t,),
        in_specs=[pl.BlockSpec((t, GLA_KEY), lambda i: (i, C_GQ // GLA_KEY)),
                  pl.BlockSpec((t, GLA_KEY), lambda i: (i, C_GK // GLA_KEY)),
                  pl.BlockSpec((t, BRANCH_W), lambda i: (i, C_GV // BRANCH_W)),
                  pl.BlockSpec((t, BRANCH_W), lambda i: (i, C_GR // BRANCH_W)),
                  pl.BlockSpec((t, A_PAD), lambda i: (i, 0)),
                  pl.BlockSpec((A_PAD, GLA_KEY), lambda i: (0, 0)),
                  pl.BlockSpec((1, GLA_KEY), lambda i: (0, 0)),
                  pl.BlockSpec((1, BRANCH_W), lambda i: (0, 0))],
        out_specs=(pl.BlockSpec((t, BRANCH_W), lambda i: (i, 0)),
                   pl.BlockSpec((GLA_HEADS, GLA_DK, GLA_DV), lambda i: (0, 0, 0))),
        scratch_shapes=[pltpu.VMEM((GLA_HEADS, GLA_DV, GLA_DK), f32),
                        pltpu.VMEM((t, GLA_KEY), f32),
                        pltpu.VMEM((t, GLA_KEY), bf16),
                        pltpu.VMEM((t, GLA_KEY), bf16),
                        pltpu.VMEM((t, GLA_KEY), f32),
                        pltpu.VMEM((t, GLA_KEY), f32),
                        pltpu.VMEM((t, BRANCH_W), f32)],
        compiler_params=_params("arbitrary"),
        name="gla_prompt",
    )(p, p, p, p, pa, wa, ba, g)


def _gla_sample_kernel(p_ref, ga_ref, s0_ref, wa_ref, ba_ref, g_ref, o_ref, s_ref):
    log_a = _gla_log_decay(ga_ref[...], wa_ref, ba_ref)
    decay = jnp.exp(log_a)
    q = p_ref[:, C_GQ:C_GQ + GLA_KEY] * (GLA_DK ** -0.5)
    k = p_ref[:, C_GK:C_GK + GLA_KEY]
    v = p_ref[:, C_GV:C_GV + BRANCH_W]
    row = lax.broadcasted_iota(jnp.int32, (8, GLA_DK), 0)
    parts = []
    for h in range(GLA_HEADS):
        kc = slice(h * GLA_DK, (h + 1) * GLA_DK)
        vc = slice(h * GLA_DV, (h + 1) * GLA_DV)
        s0 = s0_ref[h]
        score = jnp.sum(q[:, kc] * k[:, kc], axis=-1, keepdims=True)
        o = score * v[:, vc] + _dot((q[:, kc] * decay[:, kc]).astype(bf16), s0.astype(bf16))
        parts.append(o)
        tile = jnp.where(row == 0, k[:, kc], jnp.where(row == 1, decay[:, kc], 0.0))
        cols = tile.T
        s_ref[h] = cols[:, 1:2] * s0 + cols[:, 0:1] * v[:, vc]
    o_ref[...] = _gla_out_norm(jnp.concatenate(parts, axis=1), g_ref[...],
                               p_ref[:, C_GR:C_GR + BRANCH_W]).astype(bf16)


def _gla_sample(p3, pa3, state, layer, wa, ba, g):
    b = p3.shape[0]
    return pl.pallas_call(
        _gla_sample_kernel,
        out_shape=(jax.ShapeDtypeStruct((b, 1, BRANCH_W), bf16),
                   jax.ShapeDtypeStruct((b, GLA_HEADS, GLA_DK, GLA_DV), f32)),
        grid=(b,),
        in_specs=[pl.BlockSpec((None, 1, N_MAIN), lambda i: (i, 0, 0)),
                  pl.BlockSpec((None, 1, A_PAD), lambda i: (i, 0, 0)),
                  pl.BlockSpec((None, None, GLA_HEADS, GLA_DK, GLA_DV), lambda i: (layer, i, 0, 0, 0)),
                  pl.BlockSpec((A_PAD, GLA_KEY), lambda i: (0, 0)),
                  pl.BlockSpec((1, GLA_KEY), lambda i: (0, 0)),
                  pl.BlockSpec((1, BRANCH_W), lambda i: (0, 0))],
        out_specs=(pl.BlockSpec((None, 1, BRANCH_W), lambda i: (i, 0, 0)),
                   pl.BlockSpec((None, GLA_HEADS, GLA_DK, GLA_DV), lambda i: (i, 0, 0, 0))),
        compiler_params=_params("parallel"),
        name="gla_sample",
    )(p3, pa3, state, wa, ba, g)


def _merge_kernel(oa_ref, ob_ref, oc_ref, w_ref, ga_ref, gb_ref, gc_ref, o_ref):
    y = _sigmoid(ga_ref[...]) * _dot(oa_ref[...], w_ref[0])
    y = y + _sigmoid(gb_ref[...]) * _dot(ob_ref[...], w_ref[1])
    y = y + _sigmoid(gc_ref[...]) * _dot(oc_ref[...], w_ref[2])
    o_ref[...] = y.astype(bf16)


def _merge(oa, ob, oc, w_branch, p, tm, tn):
    m = oa.shape[0]
    nj = D_MODEL // tn
    branch = pl.BlockSpec((tm, BRANCH_W), lambda i, j: (i, 0))

    def gate(n):
        return pl.BlockSpec((tm, tn), lambda i, j: (i, (C_GATE + n * D_MODEL) // tn + j))

    return pl.pallas_call(
        _merge_kernel,
        out_shape=jax.ShapeDtypeStruct((m, D_MODEL), bf16),
        grid=(m // tm, nj),
        in_specs=[branch, branch, branch,
                  pl.BlockSpec((N_BRANCH, BRANCH_W, tn), lambda i, j: (0, 0, j)),
                  gate(0), gate(1), gate(2)],
        out_specs=pl.BlockSpec((tm, tn), lambda i, j: (i, j)),
        compiler_params=_params("parallel", "arbitrary"),
        name="merge",
    )(oa, ob, oc, w_branch, p, p, p)


def _matmul_residual_kernel(a_ref, w_ref, x_ref, o_ref):
    o_ref[...] = x_ref[...] + _dot(a_ref[...], w_ref[...])


def _matmul_residual(a, w, x, tm, tn):
    m, k = a.shape
    n = w.shape[1]
    return pl.pallas_call(
        _matmul_residual_kernel,
        out_shape=jax.ShapeDtypeStruct((m, n), f32),
        grid=(m // tm, n // tn),
        in_specs=[pl.BlockSpec((tm, k), lambda i, j: (i, 0)),
                  pl.BlockSpec((k, tn), lambda i, j: (0, j)),
                  pl.BlockSpec((tm, tn), lambda i, j: (i, j))],
        out_specs=pl.BlockSpec((tm, tn), lambda i, j: (i, j)),
        compiler_params=_params("parallel", "arbitrary"),
        name="matmul_residual",
    )(a, w, x)


def _ffn_in_kernel(x_ref, g_ref, wg_ref, wu_ref, o_ref, h_ref):
    @pl.when(pl.program_id(1) == 0)
    def _():
        h_ref[...] = _rms(x_ref[...], g_ref[...]).astype(bf16)

    h = h_ref[...]
    o_ref[...] = (_silu(_dot(h, wg_ref[...])) * _dot(h, wu_ref[...])).astype(bf16)


def _ffn_in(x, g, w, tm, tn):
    m, d = x.shape
    nj = D_FF // tn
    return pl.pallas_call(
        _ffn_in_kernel,
        out_shape=jax.ShapeDtypeStruct((m, D_FF), bf16),
        grid=(m // tm, nj),
        in_specs=[pl.BlockSpec((tm, d), lambda i, j: (i, 0)),
                  pl.BlockSpec((1, d), lambda i, j: (0, 0)),
                  pl.BlockSpec((d, tn), lambda i, j: (0, j)),
                  pl.BlockSpec((d, tn), lambda i, j: (0, nj + j))],
        out_specs=pl.BlockSpec((tm, tn), lambda i, j: (i, j)),
        scratch_shapes=[pltpu.VMEM((tm, d), bf16)],
        compiler_params=_params("parallel", "arbitrary"),
        name="ffn_in",
    )(x, g, w, w)


def _final_norm_kernel(x_ref, g_ref, o_ref):
    o_ref[...] = _rms(x_ref[...], g_ref[...])


def _final_norm(x, g, tm):
    m, d = x.shape
    return pl.pallas_call(
        _final_norm_kernel,
        out_shape=jax.ShapeDtypeStruct((m, d), f32),
        grid=(m // tm,),
        in_specs=[pl.BlockSpec((tm, d), lambda i: (i, 0)), pl.BlockSpec((1, d), lambda i: (0, 0))],
        out_specs=pl.BlockSpec((tm, d), lambda i: (i, 0)),
        compiler_params=_params("parallel"),
        name="final_norm",
    )(x, g)


def _layer_weights(l, g_mix, w_in, sgu_gain, w_spatial, b_spatial, sb_bias, w_gla_a2, b_gla_a, g_gla_out,
                   w_branch, w_o, g_ffn, w_ffn_in, w_ffn_out):
    a0 = C_GATE
    a1 = a0 + GLA_RANK
    wl = w_in[l]
    w_main = jnp.concatenate([wl[:, :a0], wl[:, a1:]], axis=1).astype(bf16)
    w_a = jnp.pad(wl[:, a0:a1], ((0, 0), (0, A_PAD - GLA_RANK))).astype(bf16)
    return dict(
        g_mix=g_mix[l][None, :],
        w_main=w_main,
        w_a=w_a,
        sgu_gain=sgu_gain[l][None, :],
        ws_tril=jnp.tril(w_spatial[l]).astype(bf16),
        bs_full=jnp.repeat(b_spatial[l].T, GROUP_W, axis=1),
        ws_first=jnp.repeat(w_spatial[l][:, 0, 0], GROUP_W)[None, :],
        bs_first=jnp.repeat(b_spatial[l][:, 0], GROUP_W)[None, :],
        sb_bias=sb_bias[l],
        sb_bias_col=sb_bias[l][:, None],
        w_a2=jnp.pad(w_gla_a2[l], ((0, A_PAD - GLA_RANK), (0, 0))).astype(bf16),
        b_a=b_gla_a[l][None, :],
        g_gla=g_gla_out[l][None, :],
        w_branch=w_branch[l].astype(bf16),
        w_o=w_o[l].astype(bf16),
        g_ffn=g_ffn[l][None, :],
        w_ffn_in=w_ffn_in[l].astype(bf16),
        w_ffn_out=w_ffn_out[l].astype(bf16),
    )


def _tile(m, want):
    return want if m % want == 0 else m


def _channel_mix(x, oa, ob, oc, p, w, tm):
    merged = _merge(oa, ob, oc, w["w_branch"], p, tm, 512)
    x = _matmul_residual(merged, w["w_o"], x, tm, 512)
    f = _ffn_in(x, w["g_ffn"], w["w_ffn_in"], _tile(x.shape[0], 1024), 512)
    return _matmul_residual(f, w["w_ffn_out"], x, tm, 512)


def _heads(p, col):
    return p[:, col:col + BRANCH_W].reshape(1, p.shape[0], SB_HEADS, SB_DIM)


def kernel(x_prompt, x_sample, cache_k, cache_v, state_gla, page_table, g_mix, w_in, sgu_gain, w_spatial, b_spatial, sb_bias, w_gla_a2, b_gla_a, g_gla_out, w_branch, w_o, g_ffn, w_ffn_in, w_ffn_out, g_final):
    depth = w_in.shape[0]
    batch, seq, _ = x_prompt.shape
    dec_batch, dec_seq, _ = x_sample.shape
    assert batch == 1 and dec_seq == 1
    assert cache_k.shape[2:] == (PAGE, SB_HEADS, SB_DIM)
    n_phys = cache_k.shape[1]
    ck = cache_k.reshape(depth, n_phys, PAGE * SB_HEADS, SB_DIM)
    cv = cache_v.reshape(depth, n_phys, PAGE * SB_HEADS, SB_DIM)

    xp = x_prompt.reshape(seq, D_MODEL)
    xs = x_sample.reshape(dec_batch, D_MODEL)
    tmp = _tile(seq, 1024)
    tms = dec_batch
    k_p, v_p, s_p, k_s, v_s, s_s, cv_s = [], [], [], [], [], [], []
    for l in range(depth):
        w = _layer_weights(l, g_mix, w_in, sgu_gain, w_spatial, b_spatial, sb_bias, w_gla_a2, b_gla_a, g_gla_out,
                           w_branch, w_o, g_ffn, w_ffn_in, w_ffn_out)
        p = _norm_matmul(xp, w["g_mix"], w["w_main"], tmp, 512)
        pa = _norm_matmul(xp, w["g_mix"], w["w_a"], tmp, A_PAD)
        oa = _sgu_prompt(p, w["sgu_gain"], w["ws_tril"], w["bs_full"], _tile(seq, 256))
        ob = _sb_prompt(p, w["sb_bias"])
        oc, s_new = _gla_prompt(p, pa, w["w_a2"], w["b_a"], w["g_gla"])
        xp = _channel_mix(xp, oa, ob, oc, p, w, _tile(seq, 512))
        k_p.append(_heads(p, C_SK))
        v_p.append(_heads(p, C_SV))
        s_p.append(s_new[None])
        p = _norm_matmul(xs, w["g_mix"], w["w_main"], tms, 512)
        pa = _norm_matmul(xs, w["g_mix"], w["w_a"], tms, A_PAD)
        oa, sgu_v = _sgu_sample(p, w["sgu_gain"], w["ws_first"], w["bs_first"])
        q = p[:, C_SQ:C_SQ + BRANCH_W].reshape(dec_batch, 1, BRANCH_W)
        ob = _sb_decode(q, w["sb_bias_col"], ck, cv, page_table, l).reshape(dec_batch, BRANCH_W).astype(bf16)
        oc, s_new = _gla_sample(p.reshape(dec_batch, 1, N_MAIN), pa.reshape(dec_batch, 1, A_PAD), state_gla, l,
                                w["w_a2"], w["b_a"], w["g_gla"])
        xs = _channel_mix(xs, oa, ob, oc.reshape(dec_batch, BRANCH_W), p, w, tms)
        k_s.append(p[:, C_SK:C_SK + BRANCH_W].reshape(dec_batch, 1, SB_HEADS, SB_DIM))
        v_s.append(p[:, C_SV:C_SV + BRANCH_W].reshape(dec_batch, 1, SB_HEADS, SB_DIM))
        s_s.append(s_new)
        cv_s.append(sgu_v.reshape(dec_batch, 1, BRANCH_W))

    g_fin = g_final[None, :]
    y_prompt = _final_norm(xp, g_fin, _tile(seq, 512)).reshape(1, seq, D_MODEL)
    y_sample = _final_norm(xs, g_fin, tms).reshape(dec_batch, 1, D_MODEL)
    return (y_prompt, y_sample, jnp.stack(k_p), jnp.stack(v_p), jnp.stack(k_s), jnp.stack(v_s),
            jnp.stack(s_p), jnp.stack(s_s), jnp.stack(cv_s))
```

```python
import functools

import jax
import jax.numpy as jnp
from jax import lax
from jax.experimental import pallas as pl
from jax.experimental.pallas import tpu as pltpu

f32 = jnp.float32
bf16 = jnp.bfloat16

D_MODEL = 2048
BRANCH_W = 1024
GROUP_W = 128
N_GROUPS = 8
SB_HEADS = 8
SB_DIM = 128
GLA_HEADS = 4
GLA_DK = 128
GLA_DV = 256
GLA_KEY = GLA_HEADS * GLA_DK
GLA_RANK = 16
GLA_TAU = 16.0
N_BRANCH = 3
D_FF = 5632
EPS = 1e-6
PAGE = 128
LOG2_E = 1.4426950408889634

C_AU, C_AV, C_SQ, C_SK, C_SV = 0, 1024, 2048, 3072, 4096
C_GQ, C_GK, C_GV, C_GR = 5120, 5632, 6144, 7168
C_GATE = 8192
N_MAIN = C_GATE + N_BRANCH * D_MODEL
A_PAD = 128

V7X_VMEM_LIMIT = 52 * 1024 * 1024

SB_Q_TILE = 512
SB_K_TILE = 256
SB_HEADS_PER_STEP = 2
GLA_BLOCK = 256
GLA_SUB = 16
DECODE_PAGES_PER_STEP = 8

NT_DIMS = (((1,), (1,)), ((), ()))
TN_DIMS = (((0,), (0,)), ((), ()))


def _params(*sem):
    return pltpu.CompilerParams(dimension_semantics=sem, vmem_limit_bytes=V7X_VMEM_LIMIT)


def _dot(a, b):
    return jnp.dot(a, b, preferred_element_type=f32)


def _gelu(x):
    return 0.5 * x * (1.0 + lax.erf(x * 0.7071067811865476))


def _sigmoid(x):
    return 1.0 / (1.0 + jnp.exp(-x))


def _silu(x):
    return x * _sigmoid(x)


def _softplus_neg_abs(z):
    return jnp.log1p(jnp.exp(-jnp.abs(z)))


def _split_bf16(x):
    hi = x.astype(bf16)
    lo = (x - hi.astype(f32)).astype(bf16)
    return hi, lo


def _rms(x, g):
    ms = jnp.mean(x * x, axis=-1, keepdims=True)
    return x * lax.rsqrt(ms + EPS) * g


def _norm_matmul_kernel(x_ref, g_ref, w_ref, o_ref, h_ref):
    @pl.when(pl.program_id(1) == 0)
    def _():
        h_ref[...] = _rms(x_ref[...], g_ref[...]).astype(bf16)

    o_ref[...] = _dot(h_ref[...], w_ref[...])


def _norm_matmul(x, g, w, tm, tn):
    m, d = x.shape
    n = w.shape[1]
    return pl.pallas_call(
        _norm_matmul_kernel,
        out_shape=jax.ShapeDtypeStruct((m, n), f32),
        grid=(m // tm, n // tn),
        in_specs=[pl.BlockSpec((tm, d), lambda i, j: (i, 0)),
                  pl.BlockSpec((1, d), lambda i, j: (0, 0)),
                  pl.BlockSpec((d, tn), lambda i, j: (0, j))],
        out_specs=pl.BlockSpec((tm, tn), lambda i, j: (i, j)),
        scratch_shapes=[pltpu.VMEM((tm, d), bf16)],
        compiler_params=_params("parallel", "arbitrary"),
        name="norm_matmul",
    )(x, g, w)


def _sgu_norm(av, gain):
    v = _gelu(av)
    v = v - jnp.mean(v, axis=-1, keepdims=True)
    v = v * lax.rsqrt(jnp.mean(v * v, axis=-1, keepdims=True) + EPS)
    return v * gain


def _sgu_prompt_kernel(au_ref, av_ref, gain_ref, ws_ref, bs_ref, o_ref, sv_ref):
    rows = au_ref.shape[0]
    sv_ref[...] = _sgu_norm(av_ref[...], gain_ref[...]).astype(bf16)
    for c in range(rows // GROUP_W):
        r = pl.ds(c * GROUP_W, GROUP_W)
        for g in range(N_GROUPS):
            cols = pl.ds(g * GROUP_W, GROUP_W)
            s = _dot(ws_ref[g], sv_ref[r, cols]) + bs_ref[:, cols]
            o_ref[r, cols] = (_gelu(au_ref[r, cols]) * s).astype(bf16)


def _sgu_prompt(p, gain, ws_tril, bs_full, rows):
    m = p.shape[0]
    return pl.pallas_call(
        _sgu_prompt_kernel,
        out_shape=jax.ShapeDtypeStruct((m, BRANCH_W), bf16),
        grid=(m // rows,),
        in_specs=[pl.BlockSpec((rows, BRANCH_W), lambda i: (i, C_AU // BRANCH_W)),
                  pl.BlockSpec((rows, BRANCH_W), lambda i: (i, C_AV // BRANCH_W)),
                  pl.BlockSpec((1, BRANCH_W), lambda i: (0, 0)),
                  pl.BlockSpec((N_GROUPS, GROUP_W, GROUP_W), lambda i: (0, 0, 0)),
                  pl.BlockSpec((GROUP_W, BRANCH_W), lambda i: (0, 0))],
        out_specs=pl.BlockSpec((rows, BRANCH_W), lambda i: (i, 0)),
        scratch_shapes=[pltpu.VMEM((rows, BRANCH_W), bf16)],
        compiler_params=_params("parallel"),
        name="sgu_prompt",
    )(p, p, gain, ws_tril, bs_full)


def _sgu_sample_kernel(au_ref, av_ref, gain_ref, w0_ref, b0_ref, o_ref, sv_ref):
    sv = _sgu_norm(av_ref[...], gain_ref[...])
    sv_ref[...] = sv
    o_ref[...] = (_gelu(au_ref[...]) * (w0_ref[...] * sv + b0_ref[...])).astype(bf16)


def _sgu_sample(p, gain, w0, b0):
    m = p.shape[0]
    row = pl.BlockSpec((1, BRANCH_W), lambda i: (0, 0))
    return pl.pallas_call(
        _sgu_sample_kernel,
        out_shape=(jax.ShapeDtypeStruct((m, BRANCH_W), bf16), jax.ShapeDtypeStruct((m, BRANCH_W), f32)),
        grid=(1,),
        in_specs=[pl.BlockSpec((m, BRANCH_W), lambda i: (0, C_AU // BRANCH_W)),
                  pl.BlockSpec((m, BRANCH_W), lambda i: (0, C_AV // BRANCH_W)),
                  row, row, row],
        out_specs=(pl.BlockSpec((m, BRANCH_W), lambda i: (0, 0)), pl.BlockSpec((m, BRANCH_W), lambda i: (0, 0))),
        compiler_params=_params("arbitrary"),
        name="sgu_sample",
    )(p, p, gain, w0, b0)


def _stick_logs(z2, causal=None):
    neg_abs = lax.bitcast_convert_type(lax.bitcast_convert_type(z2, jnp.uint32) | jnp.uint32(0x80000000), f32)
    sp = jnp.log2(1.0 + jnp.exp2(neg_abs))
    log_beta = jnp.minimum(z2, 0.0) - sp
    log_rest = log_beta - z2
    if causal is not None:
        log_rest = jnp.where(causal, log_rest, 0.0)
    return log_beta, log_rest


def _local_tail(log_rest, upper):
    return _dot(log_rest.astype(bf16), upper)


def _strict_upper(t):
    r = lax.broadcasted_iota(jnp.int32, (t, t), 0)
    c = lax.broadcasted_iota(jnp.int32, (t, t), 1)
    return (r > c).astype(bf16)


def _sb_prompt_kernel(bias_ref, q_ref, k_ref, v_ref, o_ref, kb_ref, vb_ref, *, tk):
    tq = q_ref.shape[0]
    hp = q_ref.shape[1] // SB_DIM
    g = pl.program_id(0)
    i = pl.program_id(1)

    @pl.when(i == 0)
    def _():
        kb_ref[...] = k_ref[...].astype(bf16)
        vb_ref[...] = v_ref[...].astype(bf16)

    scale = SB_DIM ** -0.5 * LOG2_E
    upper = _strict_upper(tk)
    q_pos = i * tq + lax.broadcasted_iota(jnp.int32, (tq, tk), 0)
    k_off = lax.broadcasted_iota(jnp.int32, (tq, tk), 1)
    heads = [pl.ds(h * SB_DIM, SB_DIM) for h in range(hp)]
    q = [q_ref[:, cols].astype(bf16) for cols in heads]
    bias = [bias_ref[g * hp + h] * LOG2_E for h in range(hp)]

    def tile(j, state, masked):
        rows = pl.ds(pl.multiple_of(j * tk, tk), tk)
        causal = (j * tk + k_off < q_pos) if masked else None
        z = [lax.dot_general(q[h], kb_ref[rows, heads[h]], NT_DIMS, preferred_element_type=f32) for h in range(hp)]
        logs = [_stick_logs(z[h] * scale + bias[h], causal) for h in range(hp)]
        tails = [_local_tail(logs[h][1], upper) for h in range(hp)]
        out = []
        for h in range(hp):
            carry, acc = state[h]
            a = jnp.exp2(logs[h][0] + tails[h] + carry)
            if masked:
                a = jnp.where(causal, a, 0.0)
            carry = carry + tails[h][:, 0:1] + logs[h][1][:, 0:1]
            out.append((carry, acc + _dot(a.astype(bf16), vb_ref[rows, heads[h]])))
        return tuple(out)

    state = tuple((jnp.zeros((tq, 1), f32), jnp.zeros((tq, SB_DIM), f32)) for _ in range(hp))
    n_diag = tq // tk
    for d in range(n_diag):
        state = tile(i * n_diag + n_diag - 1 - d, state, True)
    state = lax.fori_loop(0, i * n_diag, lambda n, s: tile(i * n_diag - 1 - n, s, False), state)
    for h in range(hp):
        o_ref[:, heads[h]] = state[h][1].astype(bf16)


def _sb_prompt(p, bias):
    m = p.shape[0]
    tq = min(SB_Q_TILE, m)
    tk = min(SB_K_TILE, m)
    w = SB_HEADS_PER_STEP * SB_DIM
    return pl.pallas_call(
        functools.partial(_sb_prompt_kernel, tk=tk),
        out_shape=jax.ShapeDtypeStruct((m, BRANCH_W), bf16),
        grid=(BRANCH_W // w, m // tq),
        in_specs=[pl.BlockSpec(memory_space=pltpu.SMEM),
                  pl.BlockSpec((tq, w), lambda g, i: (i, C_SQ // w + g)),
                  pl.BlockSpec((m, w), lambda g, i: (0, C_SK // w + g)),
                  pl.BlockSpec((m, w), lambda g, i: (0, C_SV // w + g))],
        out_specs=pl.BlockSpec((tq, w), lambda g, i: (i, g)),
        scratch_shapes=[pltpu.VMEM((m, w), bf16), pltpu.VMEM((m, w), bf16)],
        compiler_params=_params("arbitrary", "arbitrary"),
        name="sb_prompt",
    )(bias, p, p, p)


def _sb_decode_kernel(pt_ref, q_ref, bias_ref, *refs, pps):
    k_refs, v_refs = refs[:pps], refs[pps:2 * pps]
    o_ref, acc_ref, carry_ref, qc_ref = refs[2 * pps:]
    j = pl.program_id(1)

    @pl.when(j == 0)
    def _():
        acc_ref[...] = jnp.zeros_like(acc_ref)
        carry_ref[...] = jnp.zeros_like(carry_ref)
        row = lax.broadcasted_iota(jnp.int32, (PAGE, BRANCH_W), 0)
        lane_head = lax.broadcasted_iota(jnp.int32, (PAGE, BRANCH_W), 1) // SB_DIM
        q_rows = jnp.where(row == lane_head, jnp.broadcast_to(q_ref[...], (PAGE, BRANCH_W)), 0.0)
        qc_ref[...] = q_rows.T.astype(bf16)

    scale = SB_DIM ** -0.5 * LOG2_E
    upper = _strict_upper(PAGE)

    def head_major(refs_):
        return jnp.concatenate(
            [jnp.concatenate([ref[pl.ds(h, PAGE, stride=SB_HEADS), :] for h in range(SB_HEADS)], axis=1)
             for ref in refs_], axis=0).astype(bf16)

    zt = _dot(head_major(k_refs), qc_ref[...])
    z = jnp.concatenate([zt[u * PAGE:(u + 1) * PAGE, :].T[0:SB_HEADS, :] for u in range(pps)], axis=0)
    z = z * scale + bias_ref[...] * LOG2_E
    log_beta, log_rest = _stick_logs(z)
    tail = _local_tail(log_rest, upper)
    totals = tail[:, 0:1] + log_rest[:, 0:1]
    carry = carry_ref[:, 0:1]
    carries = []
    for u in range(pps):
        carries.append(carry)
        carry = carry + totals[u * SB_HEADS:(u + 1) * SB_HEADS, :]
    a = jnp.exp2(log_beta + tail + jnp.concatenate(carries, axis=0))
    a_wide = jnp.concatenate([a[u * SB_HEADS:(u + 1) * SB_HEADS, :] for u in range(pps)], axis=1)
    acc = acc_ref[...] + _dot(a_wide.astype(bf16), head_major(v_refs))
    acc_ref[...] = acc
    carry_ref[...] = jnp.broadcast_to(carry, carry_ref.shape)

    @pl.when(j == pl.num_programs(1) - 1)
    def _():
        sub = lax.broadcasted_iota(jnp.int32, (SB_HEADS, BRANCH_W), 0)
        lane_head = lax.broadcasted_iota(jnp.int32, (SB_HEADS, BRANCH_W), 1) // SB_DIM
        o_ref[...] = jnp.sum(jnp.where(sub == lane_head, acc, 0.0), axis=0, keepdims=True)


def _sb_decode(q, bias_col, cache_k, cache_v, page_table, layer):
    b, n_pages = page_table.shape
    pps = DECODE_PAGES_PER_STEP
    while n_pages % pps:
        pps //= 2
    rows = PAGE * SB_HEADS

    def page_spec(u):
        return pl.BlockSpec((None, None, rows, SB_DIM),
                            lambda s, j, pt: (layer, pt[s, n_pages - 1 - (j * pps + u)], 0, 0))

    grid_spec = pltpu.PrefetchScalarGridSpec(
        num_scalar_prefetch=1,
        grid=(b, n_pages // pps),
        in_specs=[pl.BlockSpec((None, 1, BRANCH_W), lambda s, j, pt: (s, 0, 0)),
                  pl.BlockSpec((pps * SB_HEADS, 1), lambda s, j, pt: (0, 0))]
                 + [page_spec(u) for u in range(pps)] * 2,
        out_specs=pl.BlockSpec((None, 1, BRANCH_W), lambda s, j, pt: (s, 0, 0)),
        scratch_shapes=[pltpu.VMEM((SB_HEADS, BRANCH_W), f32), pltpu.VMEM((SB_HEADS, PAGE), f32),
                        pltpu.VMEM((BRANCH_W, SB_DIM), bf16)],
    )
    return pl.pallas_call(
        functools.partial(_sb_decode_kernel, pps=pps),
        out_shape=jax.ShapeDtypeStruct((b, 1, BRANCH_W), f32),
        grid_spec=grid_spec,
        compiler_params=_params("arbitrary", "arbitrary"),
        name="sb_decode",
    )(page_table, q, jnp.tile(bias_col, (pps, 1)), *([cache_k] * pps), *([cache_v] * pps))


def _gla_log_decay(ga, wa_ref, ba_ref):
    x = _dot(ga.astype(bf16), wa_ref[...]) + ba_ref[...]
    return (jnp.minimum(x, 0.0) - _softplus_neg_abs(x)) * (1.0 / GLA_TAU)


def _gla_out_norm(o, g, r):
    parts = []
    for h in range(GLA_HEADS):
        cols = slice(h * GLA_DV, (h + 1) * GLA_DV)
        parts.append(_rms(o[:, cols], g[:, cols]))
    return jnp.concatenate(parts, axis=1) * _silu(r)


def _gla_prompt_kernel(q_ref, k_ref, v_ref, r_ref, ga_ref, wa_ref, ba_ref, g_ref, o_ref, s_ref,
                       st_ref, b_ref, qe_ref, ke_ref, eb_ref, qs_ref, oacc_ref):
    t = q_ref.shape[0]
    sub = GLA_SUB
    step = pl.program_id(0)

    @pl.when(step == 0)
    def _():
        st_ref[...] = jnp.zeros_like(st_ref)

    log_a = _gla_log_decay(ga_ref[...], wa_ref, ba_ref)
    r = lax.broadcasted_iota(jnp.int32, (t, t), 0)
    c = lax.broadcasted_iota(jnp.int32, (t, t), 1)
    same = (r // sub) == (c // sub)
    incl = (same & (c <= r)).astype(bf16)
    whole = same.astype(bf16)
    hi, lo = _split_bf16(log_a)
    b = _dot(incl, hi) + _dot(incl, lo)
    b_end = _dot(whole, hi) + _dot(whole, lo)
    qs = q_ref[...] * (GLA_DK ** -0.5)
    b_ref[...] = b
    qs_ref[...] = qs
    qe_ref[...] = (qs * jnp.exp(b)).astype(bf16)
    ke_ref[...] = (k_ref[...] * jnp.exp(b_end - b)).astype(bf16)
    eb_ref[...] = jnp.exp(b_end)

    t_idx = lax.broadcasted_iota(jnp.int32, (sub, GLA_DK), 0)

    def body(n, _):
        rows = pl.ds(pl.multiple_of(n * sub, sub), sub)
        for h in range(GLA_HEADS):
            kc = pl.ds(h * GLA_DK, GLA_DK)
            vc = pl.ds(h * GLA_DV, GLA_DV)
            st = st_ref[h]
            o = lax.dot_general(qe_ref[rows, kc], st.astype(bf16), NT_DIMS, preferred_element_type=f32)
            bi = b_ref[rows, kc]
            qi = qs_ref[rows, kc]
            ki = k_ref[rows, kc]
            vi = v_ref[rows, vc]
            for s in range(sub):
                d = bi - bi[s:s + 1, :]
                e = jnp.exp(jnp.where(t_idx >= s, d, -jnp.inf))
                sc = jnp.sum(qi * e * ki[s:s + 1, :], axis=-1, keepdims=True)
                o = o + sc * vi[s:s + 1, :]
            oacc_ref[rows, vc] = o
            upd = lax.dot_general(vi.astype(bf16), ke_ref[rows, kc], TN_DIMS, preferred_element_type=f32)
            st_ref[h] = st * eb_ref[pl.ds(pl.multiple_of(n * sub, sub), 1), kc] + upd
        return 0

    lax.fori_loop(0, t // sub, body, 0)
    o_ref[...] = _gla_out_norm(oacc_ref[...], g_ref[...], r_ref[...]).astype(bf16)

    @pl.when(step == pl.num_programs(0) - 1)
    def _():
        for h in range(GLA_HEADS):
            s_ref[h] = st_ref[h].T


def _gla_prompt(p, pa, wa, ba, g):
    m = p.shape[0]
    t = min(GLA_BLOCK, m)
    return pl.pallas_call(
        _gla_prompt_kernel,
        out_shape=(jax.ShapeDtypeStruct((m, BRANCH_W), bf16),
                   jax.ShapeDtypeStruct((GLA_HEADS, GLA_DK, GLA_DV), f32)),
        grid=(m // t,),
        in_specs=[pl.BlockSpec((t, GLA_KEY), lambda i: (i, C_GQ // GLA_KEY)),
                  pl.BlockSpec((t, GLA_KEY), lambda i: (i, C_GK // GLA_KEY)),
                  pl.BlockSpec((t, BRANCH_W), lambda i: (i, C_GV // BRANCH_W)),
                  pl.BlockSpec((t, BRANCH_W), lambda i: (i, C_GR // BRANCH_W)),
                  pl.BlockSpec((t, A_PAD), lambda i: (i, 0)),
                  pl.BlockSpec((A_PAD, GLA_KEY), lambda i: (0, 0)),
                  pl.BlockSpec((1, GLA_KEY), lambda i: (0, 0)),
                  pl.BlockSpec((1, BRANCH_W), lambda i: (0, 0))],
        out_specs=(pl.BlockSpec((t, BRANCH_W), lambda i: (i, 0)),
                   pl.BlockSpec((GLA_HEADS, GLA_DK, GLA_DV), lambda i: (0, 0, 0))),
        scratch_shapes=[pltpu.VMEM((GLA_HEADS, GLA_DV, GLA_DK), f32),
                        pltpu.VMEM((t, GLA_KEY), f32),
                        pltpu.VMEM((t, GLA_KEY), bf16),
                        pltpu.VMEM((t, GLA_KEY), bf16),
                        pltpu.VMEM((t, GLA_KEY), f32),
                        pltpu.VMEM((t, GLA_KEY), f32),
                        pltpu.VMEM((t, BRANCH_W), f32)],
        compiler_params=_params("arbitrary"),
        name="gla_prompt",
    )(p, p, p, p, pa, wa, ba, g)


def _gla_sample_kernel(p_ref, ga_ref, s0_ref, wa_ref, ba_ref, g_ref, o_ref, s_ref):
    log_a = _gla_log_decay(ga_ref[...], wa_ref, ba_ref)
    decay = jnp.exp(log_a)
    q = p_ref[:, C_GQ:C_GQ + GLA_KEY] * (GLA_DK ** -0.5)
    k = p_ref[:, C_GK:C_GK + GLA_KEY]
    v = p_ref[:, C_GV:C_GV + BRANCH_W]
    row = lax.broadcasted_iota(jnp.int32, (8, GLA_DK), 0)
    parts = []
    for h in range(GLA_HEADS):
        kc = slice(h * GLA_DK, (h + 1) * GLA_DK)
        vc = slice(h * GLA_DV, (h + 1) * GLA_DV)
        s0 = s0_ref[h]
        score = jnp.sum(q[:, kc] * k[:, kc], axis=-1, keepdims=True)
        o = score * v[:, vc] + _dot((q[:, kc] * decay[:, kc]).astype(bf16), s0.astype(bf16))
        parts.append(o)
        tile = jnp.where(row == 0, k[:, kc], jnp.where(row == 1, decay[:, kc], 0.0))
        cols = tile.T
        s_ref[h] = cols[:, 1:2] * s0 + cols[:, 0:1] * v[:, vc]
    o_ref[...] = _gla_out_norm(jnp.concatenate(parts, axis=1), g_ref[...],
                               p_ref[:, C_GR:C_GR + BRANCH_W]).astype(bf16)


def _gla_sample(p3, pa3, state, layer, wa, ba, g):
    b = p3.shape[0]
    return pl.pallas_call(
        _gla_sample_kernel,
        out_shape=(jax.ShapeDtypeStruct((b, 1, BRANCH_W), bf16),
                   jax.ShapeDtypeStruct((b, GLA_HEADS, GLA_DK, GLA_DV), f32)),
        grid=(b,),
        in_specs=[pl.BlockSpec((None, 1, N_MAIN), lambda i: (i, 0, 0)),
                  pl.BlockSpec((None, 1, A_PAD), lambda i: (i, 0, 0)),
                  pl.BlockSpec((None, None, GLA_HEADS, GLA_DK, GLA_DV), lambda i: (layer, i, 0, 0, 0)),
                  pl.BlockSpec((A_PAD, GLA_KEY), lambda i: (0, 0)),
                  pl.BlockSpec((1, GLA_KEY), lambda i: (0, 0)),
                  pl.BlockSpec((1, BRANCH_W), lambda i: (0, 0))],
        out_specs=(pl.BlockSpec((None, 1, BRANCH_W), lambda i: (i, 0, 0)),
                   pl.BlockSpec((None, GLA_HEADS, GLA_DK, GLA_DV), lambda i: (i, 0, 0, 0))),
        compiler_params=_params("parallel"),
        name="gla_sample",
    )(p3, pa3, state, wa, ba, g)


def _merge_kernel(oa_ref, ob_ref, oc_ref, w_ref, ga_ref, gb_ref, gc_ref, o_ref):
    y = _sigmoid(ga_ref[...]) * _dot(oa_ref[...], w_ref[0])
    y = y + _sigmoid(gb_ref[...]) * _dot(ob_ref[...], w_ref[1])
    y = y + _sigmoid(gc_ref[...]) * _dot(oc_ref[...], w_ref[2])
    o_ref[...] = y.astype(bf16)


def _merge(oa, ob, oc, w_branch, p, tm, tn):
    m = oa.shape[0]
    nj = D_MODEL // tn
    branch = pl.BlockSpec((tm, BRANCH_W), lambda i, j: (i, 0))

    def gate(n):
        return pl.BlockSpec((tm, tn), lambda i, j: (i, (C_GATE + n * D_MODEL) // tn + j))

    return pl.pallas_call(
        _merge_kernel,
        out_shape=jax.ShapeDtypeStruct((m, D_MODEL), bf16),
        grid=(m // tm, nj),
        in_specs=[branch, branch, branch,
                  pl.BlockSpec((N_BRANCH, BRANCH_W, tn), lambda i, j: (0, 0, j)),
                  gate(0), gate(1), gate(2)],
        out_specs=pl.BlockSpec((tm, tn), lambda i, j: (i, j)),
        compiler_params=_params("parallel", "arbitrary"),
        name="merge",
    )(oa, ob, oc, w_branch, p, p, p)


def _matmul_residual_kernel(a_ref, w_ref, x_ref, o_ref):
    o_ref[...] = x_ref[...] + _dot(a_ref[...], w_ref[...])


def _matmul_residual(a, w, x, tm, tn):
    m, k = a.shape
    n = w.shape[1]
    return pl.pallas_call(
        _matmul_residual_kernel,
        out_shape=jax.ShapeDtypeStruct((m, n), f32),
        grid=(m // tm, n // tn),
        in_specs=[pl.BlockSpec((tm, k), lambda i, j: (i, 0)),
                  pl.BlockSpec((k, tn), lambda i, j: (0, j)),
                  pl.BlockSpec((tm, tn), lambda i, j: (i, j))],
        out_specs=pl.BlockSpec((tm, tn), lambda i, j: (i, j)),
        compiler_params=_params("parallel", "arbitrary"),
        name="matmul_residual",
    )(a, w, x)


def _ffn_in_kernel(x_ref, g_ref, wg_ref, wu_ref, o_ref, h_ref):
    @pl.when(pl.program_id(1) == 0)
    def _():
        h_ref[...] = _rms(x_ref[...], g_ref[...]).astype(bf16)

    h = h_ref[...]
    o_ref[...] = (_silu(_dot(h, wg_ref[...])) * _dot(h, wu_ref[...])).astype(bf16)


def _ffn_in(x, g, w, tm, tn):
    m, d = x.shape
    nj = D_FF // tn
    return pl.pallas_call(
        _ffn_in_kernel,
        out_shape=jax.ShapeDtypeStruct((m, D_FF), bf16),
        grid=(m // tm, nj),
        in_specs=[pl.BlockSpec((tm, d), lambda i, j: (i, 0)),
                  pl.BlockSpec((1, d), lambda i, j: (0, 0)),
                  pl.BlockSpec((d, tn), lambda i, j: (0, j)),
                  pl.BlockSpec((d, tn), lambda i, j: (0, nj + j))],
        out_specs=pl.BlockSpec((tm, tn), lambda i, j: (i, j)),
        scratch_shapes=[pltpu.VMEM((tm, d), bf16)],
        compiler_params=_params("parallel", "arbitrary"),
        name="ffn_in",
    )(x, g, w, w)


def _final_norm_kernel(x_ref, g_ref, o_ref):
    o_ref[...] = _rms(x_ref[...], g_ref[...])


def _final_norm(x, g, tm):
    m, d = x.shape
    return pl.pallas_call(
        _final_norm_kernel,
        out_shape=jax.ShapeDtypeStruct((m, d), f32),
        grid=(m // tm,),
        in_specs=[pl.BlockSpec((tm, d), lambda i: (i, 0)), pl.BlockSpec((1, d), lambda i: (0, 0))],
        out_specs=pl.BlockSpec((tm, d), lambda i: (i, 0)),
        compiler_params=_params("parallel"),
        name="final_norm",
    )(x, g)


def _layer_weights(l, g_mix, w_in, sgu_gain, w_spatial, b_spatial, sb_bias, w_gla_a2, b_gla_a, g_gla_out,
                   w_branch, w_o, g_ffn, w_ffn_in, w_ffn_out):
    a0 = C_GATE
    a1 = a0 + GLA_RANK
    wl = w_in[l]
    w_main = jnp.concatenate([wl[:, :a0], wl[:, a1:]], axis=1).astype(bf16)
    w_a = jnp.pad(wl[:, a0:a1], ((0, 0), (0, A_PAD - GLA_RANK))).astype(bf16)
    return dict(
        g_mix=g_mix[l][None, :],
        w_main=w_main,
        w_a=w_a,
        sgu_gain=sgu_gain[l][None, :],
        ws_tril=jnp.tril(w_spatial[l]).astype(bf16),
        bs_full=jnp.repeat(b_spatial[l].T, GROUP_W, axis=1),
        ws_first=jnp.repeat(w_spatial[l][:, 0, 0], GROUP_W)[None, :],
        bs_first=jnp.repeat(b_spatial[l][:, 0], GROUP_W)[None, :],
        sb_bias=sb_bias[l],
        sb_bias_col=sb_bias[l][:, None],
        w_a2=jnp.pad(w_gla_a2[l], ((0, A_PAD - GLA_RANK), (0, 0))).astype(bf16),
        b_a=b_gla_a[l][None, :],
        g_gla=g_gla_out[l][None, :],
        w_branch=w_branch[l].astype(bf16),
        w_o=w_o[l].astype(bf16),
        g_ffn=g_ffn[l][None, :],
        w_ffn_in=w_ffn_in[l].astype(bf16),
        w_ffn_out=w_ffn_out[l].astype(bf16),
    )


def _tile(m, want):
    return want if m % want == 0 else m


def _channel_mix(x, oa, ob, oc, p, w):
    tm = _tile(x.shape[0], 1024)
    merged = _merge(oa, ob, oc, w["w_branch"], p, tm, 512)
    x = _matmul_residual(merged, w["w_o"], x, tm, 1024)
    f = _ffn_in(x, w["g_ffn"], w["w_ffn_in"], tm, 512)
    return _matmul_residual(f, w["w_ffn_out"], x, tm, 512)


def _heads(p, col):
    return p[:, col:col + BRANCH_W].reshape(1, p.shape[0], SB_HEADS, SB_DIM)


def kernel(x_prompt, x_sample, cache_k, cache_v, state_gla, page_table, g_mix, w_in, sgu_gain, w_spatial, b_spatial, sb_bias, w_gla_a2, b_gla_a, g_gla_out, w_branch, w_o, g_ffn, w_ffn_in, w_ffn_out, g_final):
    depth = w_in.shape[0]
    batch, seq, _ = x_prompt.shape
    dec_batch, dec_seq, _ = x_sample.shape
    assert batch == 1 and dec_seq == 1
    assert cache_k.shape[2:] == (PAGE, SB_HEADS, SB_DIM)
    n_phys = cache_k.shape[1]
    ck = cache_k.reshape(depth, n_phys, PAGE * SB_HEADS, SB_DIM)
    cv = cache_v.reshape(depth, n_phys, PAGE * SB_HEADS, SB_DIM)

    xp = x_prompt.reshape(seq, D_MODEL)
    xs = x_sample.reshape(dec_batch, D_MODEL)
    tmp = _tile(seq, 1024)
    tms = dec_batch
    k_p, v_p, s_p, k_s, v_s, s_s, cv_s = [], [], [], [], [], [], []
    for l in range(depth):
        w = _layer_weights(l, g_mix, w_in, sgu_gain, w_spatial, b_spatial, sb_bias, w_gla_a2, b_gla_a, g_gla_out,
                           w_branch, w_o, g_ffn, w_ffn_in, w_ffn_out)
        p = _norm_matmul(xp, w["g_mix"], w["w_main"], tmp, 1024)
        pa = _norm_matmul(xp, w["g_mix"], w["w_a"], tmp, A_PAD)
        oa = _sgu_prompt(p, w["sgu_gain"], w["ws_tril"], w["bs_full"], _tile(seq, 256))
        ob = _sb_prompt(p, w["sb_bias"])
        oc, s_new = _gla_prompt(p, pa, w["w_a2"], w["b_a"], w["g_gla"])
        xp = _channel_mix(xp, oa, ob, oc, p, w)
        k_p.append(_heads(p, C_SK))
        v_p.append(_heads(p, C_SV))
        s_p.append(s_new[None])
        p = _norm_matmul(xs, w["g_mix"], w["w_main"], tms, 1024)
        pa = _norm_matmul(xs, w["g_mix"], w["w_a"], tms, A_PAD)
        oa, sgu_v = _sgu_sample(p, w["sgu_gain"], w["ws_first"], w["bs_first"])
        q = p[:, C_SQ:C_SQ + BRANCH_W].reshape(dec_batch, 1, BRANCH_W)
        ob = _sb_decode(q, w["sb_bias_col"], ck, cv, page_table, l).reshape(dec_batch, BRANCH_W).astype(bf16)
        oc, s_new = _gla_sample(p.reshape(dec_batch, 1, N_MAIN), pa.reshape(dec_batch, 1, A_PAD), state_gla, l,
                                w["w_a2"], w["b_a"], w["g_gla"])
        xs = _channel_mix(xs, oa, ob, oc.reshape(dec_batch, BRANCH_W), p, w)
        k_s.append(p[:, C_SK:C_SK + BRANCH_W].reshape(dec_batch, 1, SB_HEADS, SB_DIM))
        v_s.append(p[:, C_SV:C_SV + BRANCH_W].reshape(dec_batch, 1, SB_HEADS, SB_DIM))
        s_s.append(s_new)
        cv_s.append(sgu_v.reshape(dec_batch, 1, BRANCH_W))

    g_fin = g_final[None, :]
    y_prompt = _final_norm(xp, g_fin, _tile(seq, 512)).reshape(1, seq, D_MODEL)
    y_sample = _final_norm(xs, g_fin, tms).reshape(dec_batch, 1, D_MODEL)
    return (y_prompt, y_sample, jnp.stack(k_p), jnp.stack(v_p), jnp.stack(k_s), jnp.stack(v_s),
            jnp.stack(s_p), jnp.stack(s_s), jnp.stack(cv_s))
```

```python
import functools

import jax
import jax.numpy as jnp
from jax import lax
from jax.experimental import pallas as pl
from jax.experimental.pallas import tpu as pltpu

f32 = jnp.float32
bf16 = jnp.bfloat16

D_MODEL = 2048
BRANCH_W = 1024
GROUP_W = 128
N_GROUPS = 8
SB_HEADS = 8
SB_DIM = 128
GLA_HEADS = 4
GLA_DK = 128
GLA_DV = 256
GLA_KEY = GLA_HEADS * GLA_DK
GLA_RANK = 16
GLA_TAU = 16.0
N_BRANCH = 3
D_FF = 5632
EPS = 1e-6
PAGE = 128
BF16_SUBLANES = 16
LOG2_E = 1.4426950408889634

C_AU, C_AV, C_SQ, C_SK, C_SV = 0, 1024, 2048, 3072, 4096
C_GQ, C_GK, C_GV, C_GR = 5120, 5632, 6144, 7168
C_GATE = 8192
N_MAIN = C_GATE + N_BRANCH * D_MODEL

V7X_VMEM_LIMIT = 52 * 1024 * 1024

SB_Q_TILE = 512
SB_K_TILE = 256
SB_ROW_CHUNK = 64
SB_HEADS_PER_STEP = 2
GLA_BLOCK = 256
GLA_SUB = 16
DECODE_PAGES_PER_STEP = 16

NT_DIMS = (((1,), (1,)), ((), ()))
TN_DIMS = (((0,), (0,)), ((), ()))


def _params(*sem):
    return pltpu.CompilerParams(dimension_semantics=sem, vmem_limit_bytes=V7X_VMEM_LIMIT)


def _dot(a, b):
    return jnp.dot(a, b, preferred_element_type=f32)


def _gelu(x):
    return 0.5 * x * (1.0 + lax.erf(x * 0.7071067811865476))


def _sigmoid(x):
    return 1.0 / (1.0 + jnp.exp(-x))


def _silu(x):
    return x * _sigmoid(x)


def _softplus_neg_abs(z):
    return jnp.log1p(jnp.exp(-jnp.abs(z)))


def _split_bf16(x):
    hi = x.astype(bf16)
    lo = (x - hi.astype(f32)).astype(bf16)
    return hi, lo


def _rms(x, g):
    ms = jnp.mean(x * x, axis=-1, keepdims=True)
    return x * lax.rsqrt(ms + EPS) * g


def _norm_matmul_kernel(x_ref, g_ref, w_ref, o_ref, h_ref):
    @pl.when(pl.program_id(1) == 0)
    def _():
        h_ref[...] = _rms(x_ref[...], g_ref[...]).astype(bf16)

    o_ref[...] = lax.dot_general(h_ref[...], w_ref[...], NT_DIMS, preferred_element_type=f32)


def _norm_matmul(x, g, wt, layer, row0, n_out, skip, tm, tn):
    m, d = x.shape
    gap_block, gap = skip[0] // tn, skip[1]
    return pl.pallas_call(
        _norm_matmul_kernel,
        out_shape=jax.ShapeDtypeStruct((m, n_out), f32),
        grid=(m // tm, n_out // tn),
        in_specs=[pl.BlockSpec((tm, d), lambda i, j: (i, 0)),
                  pl.BlockSpec((1, d), lambda i, j: (0, 0)),
                  pl.BlockSpec((None, pl.Element(tn), pl.Element(d)),
                               lambda i, j: (layer, pl.multiple_of(
                                   row0 + j * tn + jnp.where(j >= gap_block, gap, 0), BF16_SUBLANES), 0))],
        out_specs=pl.BlockSpec((tm, tn), lambda i, j: (i, j)),
        scratch_shapes=[pltpu.VMEM((tm, d), bf16)],
        compiler_params=_params("parallel", "arbitrary"),
        name="norm_matmul",
    )(x, g, wt)


def _sgu_norm(av, gain):
    v = _gelu(av)
    v = v - jnp.mean(v, axis=-1, keepdims=True)
    v = v * lax.rsqrt(jnp.mean(v * v, axis=-1, keepdims=True) + EPS)
    return v * gain


def _sgu_prompt_kernel(au_ref, av_ref, gain_ref, ws_ref, bs_ref, o_ref, sv_ref):
    rows = au_ref.shape[0]
    sv_ref[...] = _sgu_norm(av_ref[...], gain_ref[...]).astype(bf16)
    for c in range(rows // GROUP_W):
        r = pl.ds(c * GROUP_W, GROUP_W)
        for g in range(N_GROUPS):
            cols = pl.ds(g * GROUP_W, GROUP_W)
            s = _dot(ws_ref[g], sv_ref[r, cols]) + bs_ref[:, cols]
            o_ref[r, cols] = (_gelu(au_ref[r, cols]) * s).astype(bf16)


def _sgu_prompt(p, gain, ws_tril, bs_full, rows):
    m = p.shape[0]
    return pl.pallas_call(
        _sgu_prompt_kernel,
        out_shape=jax.ShapeDtypeStruct((m, BRANCH_W), bf16),
        grid=(m // rows,),
        in_specs=[pl.BlockSpec((rows, BRANCH_W), lambda i: (i, C_AU // BRANCH_W)),
                  pl.BlockSpec((rows, BRANCH_W), lambda i: (i, C_AV // BRANCH_W)),
                  pl.BlockSpec((1, BRANCH_W), lambda i: (0, 0)),
                  pl.BlockSpec((N_GROUPS, GROUP_W, GROUP_W), lambda i: (0, 0, 0)),
                  pl.BlockSpec((GROUP_W, BRANCH_W), lambda i: (0, 0))],
        out_specs=pl.BlockSpec((rows, BRANCH_W), lambda i: (i, 0)),
        scratch_shapes=[pltpu.VMEM((rows, BRANCH_W), bf16)],
        compiler_params=_params("parallel"),
        name="sgu_prompt",
    )(p, p, gain, ws_tril, bs_full)


def _sgu_sample_kernel(au_ref, av_ref, gain_ref, w0_ref, b0_ref, o_ref, sv_ref):
    sv = _sgu_norm(av_ref[...], gain_ref[...])
    sv_ref[...] = sv
    o_ref[...] = (_gelu(au_ref[...]) * (w0_ref[...] * sv + b0_ref[...])).astype(bf16)


def _sgu_sample(p, gain, w0, b0):
    m = p.shape[0]
    row = pl.BlockSpec((1, BRANCH_W), lambda i: (0, 0))
    return pl.pallas_call(
        _sgu_sample_kernel,
        out_shape=(jax.ShapeDtypeStruct((m, BRANCH_W), bf16), jax.ShapeDtypeStruct((m, BRANCH_W), f32)),
        grid=(1,),
        in_specs=[pl.BlockSpec((m, BRANCH_W), lambda i: (0, C_AU // BRANCH_W)),
                  pl.BlockSpec((m, BRANCH_W), lambda i: (0, C_AV // BRANCH_W)),
                  row, row, row],
        out_specs=(pl.BlockSpec((m, BRANCH_W), lambda i: (0, 0)), pl.BlockSpec((m, BRANCH_W), lambda i: (0, 0))),
        compiler_params=_params("arbitrary"),
        name="sgu_sample",
    )(p, p, gain, w0, b0)


def _stick_logs(z2, causal=None):
    sp = jnp.log2(1.0 + jnp.exp2(-jnp.abs(z2)))
    log_beta = jnp.minimum(z2, 0.0) - sp
    log_rest = log_beta - z2
    if causal is not None:
        log_rest = jnp.where(causal, log_rest, 0.0)
    return log_beta, log_rest


def _local_tail(log_rest, upper):
    return _dot(log_rest.astype(bf16), upper)


def _strict_upper(t):
    r = lax.broadcasted_iota(jnp.int32, (t, t), 0)
    c = lax.broadcasted_iota(jnp.int32, (t, t), 1)
    return (r > c).astype(bf16)


def _sb_prompt_kernel(bias_ref, q_ref, k_ref, v_ref, o_ref, kb_ref, vb_ref, *, tk):
    tq = q_ref.shape[0]
    hp = q_ref.shape[1] // SB_DIM
    g = pl.program_id(0)
    i = pl.program_id(1)

    @pl.when(i == 0)
    def _():
        kb_ref[...] = k_ref[...].astype(bf16)
        vb_ref[...] = v_ref[...].astype(bf16)

    scale = SB_DIM ** -0.5 * LOG2_E
    upper = _strict_upper(tk)
    q_pos = i * tq + lax.broadcasted_iota(jnp.int32, (tq, tk), 0)
    k_off = lax.broadcasted_iota(jnp.int32, (tq, tk), 1)
    heads = [pl.ds(h * SB_DIM, SB_DIM) for h in range(hp)]
    q = [q_ref[:, cols].astype(bf16) for cols in heads]
    bias = [bias_ref[g * hp + h] * LOG2_E for h in range(hp)]

    def tile(j, state, masked):
        rows = pl.ds(pl.multiple_of(j * tk, tk), tk)
        causal = (j * tk + k_off < q_pos) if masked else None
        chunks = [slice(r, r + SB_ROW_CHUNK) for r in range(0, tq, SB_ROW_CHUNK)]
        z = [lax.dot_general(q[h], kb_ref[rows, heads[h]], NT_DIMS, preferred_element_type=f32) for h in range(hp)]
        log_beta, rest_bf, rest_col = [], [], []
        for h in range(hp):
            parts = [_stick_logs(z[h][c] * scale + bias[h], None if causal is None else causal[c]) for c in chunks]
            log_beta.append([p[0] for p in parts])
            rest_bf.append(jnp.concatenate([p[1].astype(bf16) for p in parts], axis=0))
            rest_col.append(jnp.concatenate([p[1][:, 0:1] for p in parts], axis=0))
        tails = [_dot(rest_bf[h], upper) for h in range(hp)]
        out = []
        for h in range(hp):
            carry, acc = state[h]
            a = []
            for n, c in enumerate(chunks):
                a_c = jnp.exp2(log_beta[h][n] + tails[h][c] + carry[c])
                if masked:
                    a_c = jnp.where(causal[c], a_c, 0.0)
                a.append(a_c.astype(bf16))
            carry = carry + tails[h][:, 0:1] + rest_col[h]
            out.append((carry, acc + _dot(jnp.concatenate(a, axis=0), vb_ref[rows, heads[h]])))
        return tuple(out)

    state = tuple((jnp.zeros((tq, 1), f32), jnp.zeros((tq, SB_DIM), f32)) for _ in range(hp))
    n_diag = tq // tk
    for d in range(n_diag):
        state = tile(i * n_diag + n_diag - 1 - d, state, True)
    state = lax.fori_loop(0, i * n_diag, lambda n, s: tile(i * n_diag - 1 - n, s, False), state)
    for h in range(hp):
        o_ref[:, heads[h]] = state[h][1].astype(bf16)


def _sb_prompt(p, bias):
    m = p.shape[0]
    tq = min(SB_Q_TILE, m)
    tk = min(SB_K_TILE, m)
    w = SB_HEADS_PER_STEP * SB_DIM
    return pl.pallas_call(
        functools.partial(_sb_prompt_kernel, tk=tk),
        out_shape=jax.ShapeDtypeStruct((m, BRANCH_W), bf16),
        grid=(BRANCH_W // w, m // tq),
        in_specs=[pl.BlockSpec(memory_space=pltpu.SMEM),
                  pl.BlockSpec((tq, w), lambda g, i: (i, C_SQ // w + g)),
                  pl.BlockSpec((m, w), lambda g, i: (0, C_SK // w + g)),
                  pl.BlockSpec((m, w), lambda g, i: (0, C_SV // w + g))],
        out_specs=pl.BlockSpec((tq, w), lambda g, i: (i, g)),
        scratch_shapes=[pltpu.VMEM((m, w), bf16), pltpu.VMEM((m, w), bf16)],
        compiler_params=_params("arbitrary", "arbitrary"),
        name="sb_prompt",
    )(bias, p, p, p)


def _sb_decode_kernel(pt_ref, q_ref, bias_ref, *refs, pps):
    k_refs, v_refs = refs[:pps], refs[pps:2 * pps]
    o_ref, acc_ref, carry_ref, qc_ref = refs[2 * pps:]
    j = pl.program_id(1)

    @pl.when(j == 0)
    def _():
        acc_ref[...] = jnp.zeros_like(acc_ref)
        carry_ref[...] = jnp.zeros_like(carry_ref)
        row = lax.broadcasted_iota(jnp.int32, (PAGE, BRANCH_W), 0)
        lane_head = lax.broadcasted_iota(jnp.int32, (PAGE, BRANCH_W), 1) // SB_DIM
        q_rows = jnp.where(row == lane_head, jnp.broadcast_to(q_ref[...], (PAGE, BRANCH_W)), 0.0)
        qc_ref[...] = q_rows.T.astype(bf16)

    scale = SB_DIM ** -0.5 * LOG2_E
    upper = _strict_upper(PAGE)

    def head_major(refs_):
        return jnp.concatenate(
            [jnp.concatenate([ref[pl.ds(h, PAGE, stride=SB_HEADS), :] for h in range(SB_HEADS)], axis=1)
             for ref in refs_], axis=0).astype(bf16)

    zt = _dot(head_major(k_refs), qc_ref[...])
    z = jnp.concatenate([zt[u * PAGE:(u + 1) * PAGE, :].T[0:SB_HEADS, :] for u in range(pps)], axis=0)
    z = z * scale + bias_ref[...] * LOG2_E
    log_beta, log_rest = _stick_logs(z)
    tail = _local_tail(log_rest, upper)
    totals = tail[:, 0:1] + log_rest[:, 0:1]
    carry = carry_ref[:, 0:1]
    carries = []
    for u in range(pps):
        carries.append(carry)
        carry = carry + totals[u * SB_HEADS:(u + 1) * SB_HEADS, :]
    a = jnp.exp2(log_beta + tail + jnp.concatenate(carries, axis=0))
    a_wide = jnp.concatenate([a[u * SB_HEADS:(u + 1) * SB_HEADS, :] for u in range(pps)], axis=1)
    acc = acc_ref[...] + _dot(a_wide.astype(bf16), head_major(v_refs))
    acc_ref[...] = acc
    carry_ref[...] = jnp.broadcast_to(carry, carry_ref.shape)

    @pl.when(j == pl.num_programs(1) - 1)
    def _():
        sub = lax.broadcasted_iota(jnp.int32, (SB_HEADS, BRANCH_W), 0)
        lane_head = lax.broadcasted_iota(jnp.int32, (SB_HEADS, BRANCH_W), 1) // SB_DIM
        o_ref[...] = jnp.sum(jnp.where(sub == lane_head, acc, 0.0), axis=0, keepdims=True)


def _sb_decode(q, bias_col, cache_k, cache_v, page_table, layer):
    b, n_pages = page_table.shape
    pps = DECODE_PAGES_PER_STEP
    while n_pages % pps:
        pps //= 2
    rows = PAGE * SB_HEADS

    def page_spec(u):
        return pl.BlockSpec((None, None, rows, SB_DIM),
                            lambda s, j, pt: (layer, pt[s, n_pages - 1 - (j * pps + u)], 0, 0))

    grid_spec = pltpu.PrefetchScalarGridSpec(
        num_scalar_prefetch=1,
        grid=(b, n_pages // pps),
        in_specs=[pl.BlockSpec((None, 1, BRANCH_W), lambda s, j, pt: (s, 0, 0)),
                  pl.BlockSpec((pps * SB_HEADS, 1), lambda s, j, pt: (0, 0))]
                 + [page_spec(u) for u in range(pps)] * 2,
        out_specs=pl.BlockSpec((None, 1, BRANCH_W), lambda s, j, pt: (s, 0, 0)),
        scratch_shapes=[pltpu.VMEM((SB_HEADS, BRANCH_W), f32), pltpu.VMEM((SB_HEADS, PAGE), f32),
                        pltpu.VMEM((BRANCH_W, SB_DIM), bf16)],
    )
    return pl.pallas_call(
        functools.partial(_sb_decode_kernel, pps=pps),
        out_shape=jax.ShapeDtypeStruct((b, 1, BRANCH_W), f32),
        grid_spec=grid_spec,
        compiler_params=_params("arbitrary", "arbitrary"),
        name="sb_decode",
    )(page_table, q, jnp.tile(bias_col, (pps, 1)), *([cache_k] * pps), *([cache_v] * pps))


def _gla_log_decay(ga, wa_ref, ba_ref):
    x = _dot(ga.astype(bf16), wa_ref[...]) + ba_ref[...]
    return (jnp.minimum(x, 0.0) - _softplus_neg_abs(x)) * (1.0 / GLA_TAU)


def _gla_out_norm(o, g, r):
    parts = []
    for h in range(GLA_HEADS):
        cols = slice(h * GLA_DV, (h + 1) * GLA_DV)
        parts.append(_rms(o[:, cols], g[:, cols]))
    return jnp.concatenate(parts, axis=1) * _silu(r)


def _gla_prompt_kernel(q_ref, k_ref, v_ref, r_ref, ga_ref, wa_ref, ba_ref, g_ref, o_ref, s_ref,
                       st_ref, b_ref, qe_ref, ke_ref, eb_ref, qs_ref, oacc_ref):
    t = q_ref.shape[0]
    sub = GLA_SUB
    step = pl.program_id(0)

    @pl.when(step == 0)
    def _():
        st_ref[...] = jnp.zeros_like(st_ref)

    log_a = _gla_log_decay(ga_ref[...], wa_ref, ba_ref)
    r = lax.broadcasted_iota(jnp.int32, (t, t), 0)
    c = lax.broadcasted_iota(jnp.int32, (t, t), 1)
    same = (r // sub) == (c // sub)
    incl = (same & (c <= r)).astype(bf16)
    whole = same.astype(bf16)
    hi, lo = _split_bf16(log_a)
    b = _dot(incl, hi) + _dot(incl, lo)
    b_end = _dot(whole, hi) + _dot(whole, lo)
    qs = q_ref[...] * (GLA_DK ** -0.5)
    b_ref[...] = b
    qs_ref[...] = qs
    qe_ref[...] = (qs * jnp.exp(b)).astype(bf16)
    ke_ref[...] = (k_ref[...] * jnp.exp(b_end - b)).astype(bf16)
    eb_ref[...] = jnp.exp(b_end)

    t_idx = lax.broadcasted_iota(jnp.int32, (sub, GLA_DK), 0)

    def body(n, _):
        rows = pl.ds(pl.multiple_of(n * sub, sub), sub)
        for h in range(GLA_HEADS):
            kc = pl.ds(h * GLA_DK, GLA_DK)
            vc = pl.ds(h * GLA_DV, GLA_DV)
            st = st_ref[h]
            o = lax.dot_general(qe_ref[rows, kc], st.astype(bf16), NT_DIMS, preferred_element_type=f32)
            bi = b_ref[rows, kc]
            qi = qs_ref[rows, kc]
            ki = k_ref[rows, kc]
            vi = v_ref[rows, vc]
            for s in range(sub):
                d = bi - bi[s:s + 1, :]
                e = jnp.exp(jnp.where(t_idx >= s, d, -jnp.inf))
                sc = jnp.sum(qi * e * ki[s:s + 1, :], axis=-1, keepdims=True)
                o = o + sc * vi[s:s + 1, :]
            oacc_ref[rows, vc] = o
            upd = lax.dot_general(vi.astype(bf16), ke_ref[rows, kc], TN_DIMS, preferred_element_type=f32)
            st_ref[h] = st * eb_ref[pl.ds(pl.multiple_of(n * sub, sub), 1), kc] + upd
        return 0

    lax.fori_loop(0, t // sub, body, 0)
    o_ref[...] = _gla_out_norm(oacc_ref[...], g_ref[...], r_ref[...]).astype(bf16)

    @pl.when(step == pl.num_programs(0) - 1)
    def _():
        for h in range(GLA_HEADS):
            s_ref[h] = st_ref[h].T


def _gla_prompt(p, pa, wa, ba, g):
    m = p.shape[0]
    t = min(GLA_BLOCK, m)
    return pl.pallas_call(
        _gla_prompt_kernel,
        out_shape=(jax.ShapeDtypeStruct((m, BRANCH_W), bf16),
                   jax.ShapeDtypeStruct((GLA_HEADS, GLA_DK, GLA_DV), f32)),
        grid=(m // t,),
        in_specs=[pl.BlockSpec((t, GLA_KEY), lambda i: (i, C_GQ // GLA_KEY)),
                  pl.BlockSpec((t, GLA_KEY), lambda i: (i, C_GK // GLA_KEY)),
                  pl.BlockSpec((t, BRANCH_W), lambda i: (i, C_GV // BRANCH_W)),
                  pl.BlockSpec((t, BRANCH_W), lambda i: (i, C_GR // BRANCH_W)),
                  pl.BlockSpec((t, GLA_RANK), lambda i: (i, 0)),
                  pl.BlockSpec((GLA_RANK, GLA_KEY), lambda i: (0, 0)),
                  pl.BlockSpec((1, GLA_KEY), lambda i: (0, 0)),
                  pl.BlockSpec((1, BRANCH_W), lambda i: (0, 0))],
        out_specs=(pl.BlockSpec((t, BRANCH_W), lambda i: (i, 0)),
                   pl.BlockSpec((GLA_HEADS, GLA_DK, GLA_DV), lambda i: (0, 0, 0))),
        scratch_shapes=[pltpu.VMEM((GLA_HEADS, GLA_DV, GLA_DK), f32),
                        pltpu.VMEM((t, GLA_KEY), f32),
                        pltpu.VMEM((t, GLA_KEY), bf16),
                        pltpu.VMEM((t, GLA_KEY), bf16),
                        pltpu.VMEM((t, GLA_KEY), f32),
                        pltpu.VMEM((t, GLA_KEY), f32),
                        pltpu.VMEM((t, BRANCH_W), f32)],
        compiler_params=_params("arbitrary"),
        name="gla_prompt",
    )(p, p, p, p, pa, wa, ba, g)


def _gla_sample_kernel(p_ref, ga_ref, s0_ref, wa_ref, ba_ref, g_ref, o_ref, s_ref):
    log_a = _gla_log_decay(ga_ref[...], wa_ref, ba_ref)
    decay = jnp.exp(log_a)
    q = p_ref[:, C_GQ:C_GQ + GLA_KEY] * (GLA_DK ** -0.5)
    k = p_ref[:, C_GK:C_GK + GLA_KEY]
    v = p_ref[:, C_GV:C_GV + BRANCH_W]
    row = lax.broadcasted_iota(jnp.int32, (8, GLA_DK), 0)
    parts = []
    for h in range(GLA_HEADS):
        kc = slice(h * GLA_DK, (h + 1) * GLA_DK)
        vc = slice(h * GLA_DV, (h + 1) * GLA_DV)
        s0 = s0_ref[h]
        score = jnp.sum(q[:, kc] * k[:, kc], axis=-1, keepdims=True)
        o = score * v[:, vc] + _dot((q[:, kc] * decay[:, kc]).astype(bf16), s0.astype(bf16))
        parts.append(o)
        tile = jnp.where(row == 0, k[:, kc], jnp.where(row == 1, decay[:, kc], 0.0))
        cols = tile.T
        s_ref[h] = cols[:, 1:2] * s0 + cols[:, 0:1] * v[:, vc]
    o_ref[...] = _gla_out_norm(jnp.concatenate(parts, axis=1), g_ref[...],
                               p_ref[:, C_GR:C_GR + BRANCH_W]).astype(bf16)


def _gla_sample(p3, pa3, state, layer, wa, ba, g):
    b = p3.shape[0]
    return pl.pallas_call(
        _gla_sample_kernel,
        out_shape=(jax.ShapeDtypeStruct((b, 1, BRANCH_W), bf16),
                   jax.ShapeDtypeStruct((b, GLA_HEADS, GLA_DK, GLA_DV), f32)),
        grid=(b,),
        in_specs=[pl.BlockSpec((None, 1, N_MAIN), lambda i: (i, 0, 0)),
                  pl.BlockSpec((None, 1, GLA_RANK), lambda i: (i, 0, 0)),
                  pl.BlockSpec((None, None, GLA_HEADS, GLA_DK, GLA_DV), lambda i: (layer, i, 0, 0, 0)),
                  pl.BlockSpec((GLA_RANK, GLA_KEY), lambda i: (0, 0)),
                  pl.BlockSpec((1, GLA_KEY), lambda i: (0, 0)),
                  pl.BlockSpec((1, BRANCH_W), lambda i: (0, 0))],
        out_specs=(pl.BlockSpec((None, 1, BRANCH_W), lambda i: (i, 0, 0)),
                   pl.BlockSpec((None, GLA_HEADS, GLA_DK, GLA_DV), lambda i: (i, 0, 0, 0))),
        compiler_params=_params("parallel"),
        name="gla_sample",
    )(p3, pa3, state, wa, ba, g)


def _merge_kernel(oa_ref, ob_ref, oc_ref, w_ref, ga_ref, gb_ref, gc_ref, o_ref):
    y = _sigmoid(ga_ref[...]) * _dot(oa_ref[...], w_ref[0])
    y = y + _sigmoid(gb_ref[...]) * _dot(ob_ref[...], w_ref[1])
    y = y + _sigmoid(gc_ref[...]) * _dot(oc_ref[...], w_ref[2])
    o_ref[...] = y.astype(bf16)


def _merge(oa, ob, oc, w_branch, layer, p, tm, tn):
    m = oa.shape[0]
    nj = D_MODEL // tn
    branch = pl.BlockSpec((tm, BRANCH_W), lambda i, j: (i, 0))

    def gate(n):
        return pl.BlockSpec((tm, tn), lambda i, j: (i, (C_GATE + n * D_MODEL) // tn + j))

    return pl.pallas_call(
        _merge_kernel,
        out_shape=jax.ShapeDtypeStruct((m, D_MODEL), bf16),
        grid=(m // tm, nj),
        in_specs=[branch, branch, branch,
                  pl.BlockSpec((None, N_BRANCH, BRANCH_W, tn), lambda i, j: (layer, 0, 0, j)),
                  gate(0), gate(1), gate(2)],
        out_specs=pl.BlockSpec((tm, tn), lambda i, j: (i, j)),
        compiler_params=_params("parallel", "arbitrary"),
        name="merge",
    )(oa, ob, oc, w_branch, p, p, p)


def _matmul_residual_kernel(a_ref, w_ref, x_ref, o_ref):
    o_ref[...] = x_ref[...] + _dot(a_ref[...], w_ref[...])


def _matmul_residual(a, w, layer, x, tm, tn):
    m, k = a.shape
    n = w.shape[2]
    return pl.pallas_call(
        _matmul_residual_kernel,
        out_shape=jax.ShapeDtypeStruct((m, n), f32),
        grid=(m // tm, n // tn),
        in_specs=[pl.BlockSpec((tm, k), lambda i, j: (i, 0)),
                  pl.BlockSpec((None, k, tn), lambda i, j: (layer, 0, j)),
                  pl.BlockSpec((tm, tn), lambda i, j: (i, j))],
        out_specs=pl.BlockSpec((tm, tn), lambda i, j: (i, j)),
        compiler_params=_params("parallel", "arbitrary"),
        name="matmul_residual",
    )(a, w, x)


def _ffn_in_kernel(x_ref, g_ref, wg_ref, wu_ref, o_ref, h_ref):
    @pl.when(pl.program_id(1) == 0)
    def _():
        h_ref[...] = _rms(x_ref[...], g_ref[...]).astype(bf16)

    h = h_ref[...]
    o_ref[...] = (_silu(_dot(h, wg_ref[...])) * _dot(h, wu_ref[...])).astype(bf16)


def _ffn_in(x, g, w, layer, tm, tn):
    m, d = x.shape
    nj = D_FF // tn
    return pl.pallas_call(
        _ffn_in_kernel,
        out_shape=jax.ShapeDtypeStruct((m, D_FF), bf16),
        grid=(m // tm, nj),
        in_specs=[pl.BlockSpec((tm, d), lambda i, j: (i, 0)),
                  pl.BlockSpec((1, d), lambda i, j: (0, 0)),
                  pl.BlockSpec((None, d, tn), lambda i, j: (layer, 0, j)),
                  pl.BlockSpec((None, d, tn), lambda i, j: (layer, 0, nj + j))],
        out_specs=pl.BlockSpec((tm, tn), lambda i, j: (i, j)),
        scratch_shapes=[pltpu.VMEM((tm, d), bf16)],
        compiler_params=_params("parallel", "arbitrary"),
        name="ffn_in",
    )(x, g, w, w)


def _final_norm_kernel(x_ref, g_ref, o_ref):
    o_ref[...] = _rms(x_ref[...], g_ref[...])


def _final_norm(x, g, tm):
    m, d = x.shape
    return pl.pallas_call(
        _final_norm_kernel,
        out_shape=jax.ShapeDtypeStruct((m, d), f32),
        grid=(m // tm,),
        in_specs=[pl.BlockSpec((tm, d), lambda i: (i, 0)), pl.BlockSpec((1, d), lambda i: (0, 0))],
        out_specs=pl.BlockSpec((tm, d), lambda i: (i, 0)),
        compiler_params=_params("parallel"),
        name="final_norm",
    )(x, g)


def _matmul_weights(w_in, w_branch, w_o, w_ffn_in, w_ffn_out):
    return dict(
        w_in_t=jnp.swapaxes(w_in, 1, 2).astype(bf16),
        w_branch=w_branch.astype(bf16),
        w_o=w_o.astype(bf16),
        w_ffn_in=w_ffn_in.astype(bf16),
        w_ffn_out=w_ffn_out.astype(bf16),
    )


def _layer_params(l, g_mix, sgu_gain, w_spatial, b_spatial, sb_bias, w_gla_a2, b_gla_a, g_gla_out, g_ffn):
    return dict(
        g_mix=g_mix[l][None, :],
        sgu_gain=sgu_gain[l][None, :],
        ws_tril=jnp.tril(w_spatial[l]).astype(bf16),
        bs_full=jnp.repeat(b_spatial[l].T, GROUP_W, axis=1),
        ws_first=jnp.repeat(w_spatial[l][:, 0, 0], GROUP_W)[None, :],
        bs_first=jnp.repeat(b_spatial[l][:, 0], GROUP_W)[None, :],
        sb_bias=sb_bias[l],
        sb_bias_col=sb_bias[l][:, None],
        w_a2=w_gla_a2[l].astype(bf16),
        b_a=b_gla_a[l][None, :],
        g_gla=g_gla_out[l][None, :],
        g_ffn=g_ffn[l][None, :],
    )


def _tile(m, want):
    return want if m % want == 0 else m


def _in_projection(x, w, mw, l):
    tm = _tile(x.shape[0], 1024)
    p = _norm_matmul(x, w["g_mix"], mw["w_in_t"], l, 0, N_MAIN, (C_GATE, GLA_RANK), tm, 1024)
    pa = _norm_matmul(x, w["g_mix"], mw["w_in_t"], l, C_GATE, GLA_RANK, (GLA_RANK, 0), tm, GLA_RANK)
    return p, pa


def _channel_mix(x, oa, ob, oc, p, w, mw, l):
    tm = _tile(x.shape[0], 1024)
    merged = _merge(oa, ob, oc, mw["w_branch"], l, p, tm, 512)
    x = _matmul_residual(merged, mw["w_o"], l, x, tm, 1024)
    f = _ffn_in(x, w["g_ffn"], mw["w_ffn_in"], l, tm, 512)
    return _matmul_residual(f, mw["w_ffn_out"], l, x, tm, 512)


def _heads(p, col):
    return p[:, col:col + BRANCH_W].reshape(1, p.shape[0], SB_HEADS, SB_DIM)


def kernel(x_prompt, x_sample, cache_k, cache_v, state_gla, page_table, g_mix, w_in, sgu_gain, w_spatial, b_spatial, sb_bias, w_gla_a2, b_gla_a, g_gla_out, w_branch, w_o, g_ffn, w_ffn_in, w_ffn_out, g_final):
    depth = w_in.shape[0]
    batch, seq, _ = x_prompt.shape
    dec_batch, dec_seq, _ = x_sample.shape
    assert batch == 1 and dec_seq == 1
    assert cache_k.shape[2:] == (PAGE, SB_HEADS, SB_DIM)
    n_phys = cache_k.shape[1]
    ck = cache_k.reshape(depth, n_phys, PAGE * SB_HEADS, SB_DIM)
    cv = cache_v.reshape(depth, n_phys, PAGE * SB_HEADS, SB_DIM)

    xp = x_prompt.reshape(seq, D_MODEL)
    xs = x_sample.reshape(dec_batch, D_MODEL)
    tms = dec_batch
    mw = _matmul_weights(w_in, w_branch, w_o, w_ffn_in, w_ffn_out)
    k_p, v_p, s_p, k_s, v_s, s_s, cv_s = [], [], [], [], [], [], []
    for l in range(depth):
        w = _layer_params(l, g_mix, sgu_gain, w_spatial, b_spatial, sb_bias, w_gla_a2, b_gla_a, g_gla_out, g_ffn)
        p, pa = _in_projection(xp, w, mw, l)
        oa = _sgu_prompt(p, w["sgu_gain"], w["ws_tril"], w["bs_full"], _tile(seq, 256))
        ob = _sb_prompt(p, w["sb_bias"])
        oc, s_new = _gla_prompt(p, pa, w["w_a2"], w["b_a"], w["g_gla"])
        xp = _channel_mix(xp, oa, ob, oc, p, w, mw, l)
        k_p.append(_heads(p, C_SK))
        v_p.append(_heads(p, C_SV))
        s_p.append(s_new[None])
        p, pa = _in_projection(xs, w, mw, l)
        oa, sgu_v = _sgu_sample(p, w["sgu_gain"], w["ws_first"], w["bs_first"])
        q = p[:, C_SQ:C_SQ + BRANCH_W].reshape(dec_batch, 1, BRANCH_W)
        ob = _sb_decode(q, w["sb_bias_col"], ck, cv, page_table, l).reshape(dec_batch, BRANCH_W).astype(bf16)
        oc, s_new = _gla_sample(p.reshape(dec_batch, 1, N_MAIN), pa.reshape(dec_batch, 1, GLA_RANK), state_gla, l,
                                w["w_a2"], w["b_a"], w["g_gla"])
        xs = _channel_mix(xs, oa, ob, oc.reshape(dec_batch, BRANCH_W), p, w, mw, l)
        k_s.append(p[:, C_SK:C_SK + BRANCH_W].reshape(dec_batch, 1, SB_HEADS, SB_DIM))
        v_s.append(p[:, C_SV:C_SV + BRANCH_W].reshape(dec_batch, 1, SB_HEADS, SB_DIM))
        s_s.append(s_new)
        cv_s.append(sgu_v.reshape(dec_batch, 1, BRANCH_W))

    g_fin = g_final[None, :]
    y_prompt = _final_norm(xp, g_fin, _tile(seq, 512)).reshape(1, seq, D_MODEL)
    y_sample = _final_norm(xs, g_fin, tms).reshape(dec_batch, 1, D_MODEL)
    return (y_prompt, y_sample, jnp.stack(k_p), jnp.stack(v_p), jnp.stack(k_s), jnp.stack(v_s),
            jnp.stack(s_p), jnp.stack(s_s), jnp.stack(cv_s))
```

```python
import functools

import jax
import jax.numpy as jnp
from jax import lax
from jax.experimental import pallas as pl
from jax.experimental.pallas import tpu as pltpu

f32 = jnp.float32
bf16 = jnp.bfloat16

D_MODEL = 2048
BRANCH_W = 1024
GROUP_W = 128
N_GROUPS = 8
SB_HEADS = 8
SB_DIM = 128
GLA_HEADS = 4
GLA_DK = 128
GLA_DV = 256
GLA_KEY = GLA_HEADS * GLA_DK
GLA_RANK = 16
GLA_TAU = 16.0
N_BRANCH = 3
D_FF = 5632
EPS = 1e-6
PAGE = 128
BF16_SUBLANES = 16
LOG2_E = 1.4426950408889634

C_AU, C_AV, C_SQ, C_SK, C_SV = 0, 1024, 2048, 3072, 4096
C_GQ, C_GK, C_GV, C_GR = 5120, 5632, 6144, 7168
C_GATE = 8192
N_MAIN = C_GATE + N_BRANCH * D_MODEL

V7X_VMEM_LIMIT = 52 * 1024 * 1024

SB_Q_TILE = 512
SB_K_TILE = 256
SB_ROW_CHUNK = 64
SB_HEADS_PER_STEP = 2
GLA_BLOCK = 256
GLA_SUB = 16
DECODE_PAGES_PER_STEP = 16

NT_DIMS = (((1,), (1,)), ((), ()))
TN_DIMS = (((0,), (0,)), ((), ()))


def _params(*sem):
    return pltpu.CompilerParams(dimension_semantics=sem, vmem_limit_bytes=V7X_VMEM_LIMIT)


def _dot(a, b):
    return jnp.dot(a, b, preferred_element_type=f32)


def _gelu(x):
    return 0.5 * x * (1.0 + lax.erf(x * 0.7071067811865476))


def _sigmoid(x):
    return 1.0 / (1.0 + jnp.exp(-x))


def _silu(x):
    return x * _sigmoid(x)


def _softplus_neg_abs(z):
    return jnp.log1p(jnp.exp(-jnp.abs(z)))


def _split_bf16(x):
    hi = x.astype(bf16)
    lo = (x - hi.astype(f32)).astype(bf16)
    return hi, lo


def _rms(x, g):
    ms = jnp.mean(x * x, axis=-1, keepdims=True)
    return x * lax.rsqrt(ms + EPS) * g


def _in_projection_kernel(x_ref, g_ref, w_ref, ws_ref, o_ref, os_ref, h_ref):
    @pl.when(pl.program_id(1) == 0)
    def _():
        h = _rms(x_ref[...], g_ref[...]).astype(bf16)
        h_ref[...] = h
        os_ref[...] = lax.dot_general(h, ws_ref[...], NT_DIMS, preferred_element_type=f32)

    o_ref[...] = lax.dot_general(h_ref[...], w_ref[...], NT_DIMS, preferred_element_type=f32)


def _in_projection_call(x, g, wt, layer, gap_row, gap, tm, tn):
    m, d = x.shape
    n_out = wt.shape[1] - gap
    gap_block = gap_row // tn

    def w_rows(i, j):
        return layer, pl.multiple_of(j * tn + jnp.where(j >= gap_block, gap, 0), BF16_SUBLANES), 0

    return pl.pallas_call(
        _in_projection_kernel,
        out_shape=(jax.ShapeDtypeStruct((m, n_out), f32), jax.ShapeDtypeStruct((m, gap), f32)),
        grid=(m // tm, n_out // tn),
        in_specs=[pl.BlockSpec((tm, d), lambda i, j: (i, 0)),
                  pl.BlockSpec((1, d), lambda i, j: (0, 0)),
                  pl.BlockSpec((None, pl.Element(tn), pl.Element(d)), w_rows),
                  pl.BlockSpec((None, pl.Element(gap), pl.Element(d)), lambda i, j: (layer, gap_row, 0))],
        out_specs=(pl.BlockSpec((tm, tn), lambda i, j: (i, j)), pl.BlockSpec((tm, gap), lambda i, j: (i, 0))),
        scratch_shapes=[pltpu.VMEM((tm, d), bf16)],
        compiler_params=_params("parallel", "arbitrary"),
        name="in_projection",
    )(x, g, wt, wt)


def _sgu_norm(av, gain):
    v = _gelu(av)
    v = v - jnp.mean(v, axis=-1, keepdims=True)
    v = v * lax.rsqrt(jnp.mean(v * v, axis=-1, keepdims=True) + EPS)
    return v * gain


def _sgu_prompt_kernel(au_ref, av_ref, gain_ref, ws_ref, bs_ref, o_ref, sv_ref):
    rows = au_ref.shape[0]
    sv_ref[...] = _sgu_norm(av_ref[...], gain_ref[...]).astype(bf16)
    for c in range(rows // GROUP_W):
        r = pl.ds(c * GROUP_W, GROUP_W)
        for g in range(N_GROUPS):
            cols = pl.ds(g * GROUP_W, GROUP_W)
            s = _dot(ws_ref[g], sv_ref[r, cols]) + bs_ref[:, cols]
            o_ref[r, cols] = (_gelu(au_ref[r, cols]) * s).astype(bf16)


def _sgu_prompt(p, gain, ws_tril, bs_full, rows):
    m = p.shape[0]
    return pl.pallas_call(
        _sgu_prompt_kernel,
        out_shape=jax.ShapeDtypeStruct((m, BRANCH_W), bf16),
        grid=(m // rows,),
        in_specs=[pl.BlockSpec((rows, BRANCH_W), lambda i: (i, C_AU // BRANCH_W)),
                  pl.BlockSpec((rows, BRANCH_W), lambda i: (i, C_AV // BRANCH_W)),
                  pl.BlockSpec((1, BRANCH_W), lambda i: (0, 0)),
                  pl.BlockSpec((N_GROUPS, GROUP_W, GROUP_W), lambda i: (0, 0, 0)),
                  pl.BlockSpec((GROUP_W, BRANCH_W), lambda i: (0, 0))],
        out_specs=pl.BlockSpec((rows, BRANCH_W), lambda i: (i, 0)),
        scratch_shapes=[pltpu.VMEM((rows, BRANCH_W), bf16)],
        compiler_params=_params("parallel"),
        name="sgu_prompt",
    )(p, p, gain, ws_tril, bs_full)


def _sgu_sample_kernel(au_ref, av_ref, gain_ref, w0_ref, b0_ref, o_ref, sv_ref):
    sv = _sgu_norm(av_ref[...], gain_ref[...])
    sv_ref[...] = sv
    o_ref[...] = (_gelu(au_ref[...]) * (w0_ref[...] * sv + b0_ref[...])).astype(bf16)


def _sgu_sample(p, gain, w0, b0):
    m = p.shape[0]
    row = pl.BlockSpec((1, BRANCH_W), lambda i: (0, 0))
    return pl.pallas_call(
        _sgu_sample_kernel,
        out_shape=(jax.ShapeDtypeStruct((m, BRANCH_W), bf16), jax.ShapeDtypeStruct((m, BRANCH_W), f32)),
        grid=(1,),
        in_specs=[pl.BlockSpec((m, BRANCH_W), lambda i: (0, C_AU // BRANCH_W)),
                  pl.BlockSpec((m, BRANCH_W), lambda i: (0, C_AV // BRANCH_W)),
                  row, row, row],
        out_specs=(pl.BlockSpec((m, BRANCH_W), lambda i: (0, 0)), pl.BlockSpec((m, BRANCH_W), lambda i: (0, 0))),
        compiler_params=_params("arbitrary"),
        name="sgu_sample",
    )(p, p, gain, w0, b0)


def _stick_logs(z2, causal=None):
    sp = jnp.log2(1.0 + jnp.exp2(-jnp.abs(z2)))
    log_beta = jnp.minimum(z2, 0.0) - sp
    log_rest = log_beta - z2
    if causal is not None:
        log_rest = jnp.where(causal, log_rest, 0.0)
    return log_beta, log_rest


def _local_tail(log_rest, upper):
    return _dot(log_rest.astype(bf16), upper)


def _strict_upper(t):
    r = lax.broadcasted_iota(jnp.int32, (t, t), 0)
    c = lax.broadcasted_iota(jnp.int32, (t, t), 1)
    return (r > c).astype(bf16)


def _sb_prompt_kernel(bias_ref, q_ref, k_ref, v_ref, o_ref, kb_ref, vb_ref, lb_ref, tail_ref, col_ref, *, tk):
    tq = q_ref.shape[0]
    hp = q_ref.shape[1] // SB_DIM
    g = pl.program_id(0)
    i = pl.program_id(1)

    @pl.when(i == 0)
    def _():
        kb_ref[...] = k_ref[...].astype(bf16)
        vb_ref[...] = v_ref[...].astype(bf16)

    scale = SB_DIM ** -0.5 * LOG2_E
    upper = _strict_upper(tk)
    q_pos = i * tq + lax.broadcasted_iota(jnp.int32, (tq, tk), 0)
    k_off = lax.broadcasted_iota(jnp.int32, (tq, tk), 1)
    heads = [pl.ds(h * SB_DIM, SB_DIM) for h in range(hp)]
    q = [q_ref[:, cols].astype(bf16) for cols in heads]
    bias = [bias_ref[g * hp + h] * LOG2_E for h in range(hp)]

    chunks = [slice(r, r + SB_ROW_CHUNK) for r in range(0, tq, SB_ROW_CHUNK)]

    def key_rows(j):
        return pl.ds(pl.multiple_of(j * tk, tk), tk)

    def logits(j):
        return [lax.dot_general(q[h], kb_ref[key_rows(j), heads[h]], NT_DIMS, preferred_element_type=f32)
                for h in range(hp)]

    def finish_stage_a(z, j, masked):
        causal = (j * tk + k_off < q_pos) if masked else None
        rest_bf = []
        for h in range(hp):
            parts = []
            for c in chunks:
                log_beta, log_rest = _stick_logs(z[h][c] * scale + bias[h], None if causal is None else causal[c])
                if masked:
                    log_beta = jnp.where(causal[c], log_beta, -jnp.inf)
                lb_ref[h, c, :] = log_beta
                col_ref[h, c, :] = log_rest[:, 0:1]
                parts.append(log_rest.astype(bf16))
            rest_bf.append(jnp.concatenate(parts, axis=0))
        for h in range(hp):
            tail_ref[h] = _dot(rest_bf[h], upper)

    def stage_b(state, j):
        out = []
        for h in range(hp):
            carry, acc = state[h]
            a = [jnp.exp2(lb_ref[h, c, :] + tail_ref[h, c, :] + carry[c]).astype(bf16) for c in chunks]
            carry = carry + tail_ref[h, :, 0:1] + col_ref[h]
            out.append((carry, acc + _dot(jnp.concatenate(a, axis=0), vb_ref[key_rows(j), heads[h]])))
        return tuple(out)

    def step(j_next, masked, state, j):
        z = logits(j_next)
        state = stage_b(state, j)
        finish_stage_a(z, j_next, masked)
        return state

    state = tuple((jnp.zeros((tq, 1), f32), jnp.zeros((tq, SB_DIM), f32)) for _ in range(hp))
    n_diag = tq // tk
    first = i * n_diag + n_diag - 1
    finish_stage_a(logits(first), first, True)
    for d in range(1, n_diag):
        state = step(first - d, True, state, first - d + 1)
    state = lax.fori_loop(0, i * n_diag, lambda n, s: step(i * n_diag - 1 - n, False, s, i * n_diag - n), state)
    state = stage_b(state, 0)
    for h in range(hp):
        o_ref[:, heads[h]] = state[h][1].astype(bf16)


def _sb_prompt(p, bias):
    m = p.shape[0]
    tq = min(SB_Q_TILE, m)
    tk = min(SB_K_TILE, m)
    w = SB_HEADS_PER_STEP * SB_DIM
    return pl.pallas_call(
        functools.partial(_sb_prompt_kernel, tk=tk),
        out_shape=jax.ShapeDtypeStruct((m, BRANCH_W), bf16),
        grid=(BRANCH_W // w, m // tq),
        in_specs=[pl.BlockSpec(memory_space=pltpu.SMEM),
                  pl.BlockSpec((tq, w), lambda g, i: (i, C_SQ // w + g)),
                  pl.BlockSpec((m, w), lambda g, i: (0, C_SK // w + g)),
                  pl.BlockSpec((m, w), lambda g, i: (0, C_SV // w + g))],
        out_specs=pl.BlockSpec((tq, w), lambda g, i: (i, g)),
        scratch_shapes=[pltpu.VMEM((m, w), bf16), pltpu.VMEM((m, w), bf16),
                        pltpu.VMEM((SB_HEADS_PER_STEP, tq, tk), f32), pltpu.VMEM((SB_HEADS_PER_STEP, tq, tk), f32),
                        pltpu.VMEM((SB_HEADS_PER_STEP, tq, 1), f32)],
        compiler_params=_params("arbitrary", "arbitrary"),
        name="sb_prompt",
    )(bias, p, p, p)


def _sb_decode_kernel(pt_ref, q_ref, bias_ref, *refs, pps):
    k_refs, v_refs = refs[:pps], refs[pps:2 * pps]
    o_ref, acc_ref, carry_ref, qc_ref = refs[2 * pps:]
    j = pl.program_id(1)

    @pl.when(j == 0)
    def _():
        acc_ref[...] = jnp.zeros_like(acc_ref)
        carry_ref[...] = jnp.zeros_like(carry_ref)
        row = lax.broadcasted_iota(jnp.int32, (PAGE, BRANCH_W), 0)
        lane_head = lax.broadcasted_iota(jnp.int32, (PAGE, BRANCH_W), 1) // SB_DIM
        q_rows = jnp.where(row == lane_head, jnp.broadcast_to(q_ref[...], (PAGE, BRANCH_W)), 0.0)
        qc_ref[...] = q_rows.T.astype(bf16)

    scale = SB_DIM ** -0.5 * LOG2_E
    upper = _strict_upper(PAGE)

    def head_major(refs_):
        return jnp.concatenate(
            [jnp.concatenate([ref[pl.ds(h, PAGE, stride=SB_HEADS), :] for h in range(SB_HEADS)], axis=1)
             for ref in refs_], axis=0).astype(bf16)

    zt = _dot(head_major(k_refs), qc_ref[...])
    z = jnp.concatenate([zt[u * PAGE:(u + 1) * PAGE, :].T[0:SB_HEADS, :] for u in range(pps)], axis=0)
    z = z * scale + bias_ref[...] * LOG2_E
    log_beta, log_rest = _stick_logs(z)
    tail = _local_tail(log_rest, upper)
    totals = tail[:, 0:1] + log_rest[:, 0:1]
    carry = carry_ref[:, 0:1]
    carries = []
    for u in range(pps):
        carries.append(carry)
        carry = carry + totals[u * SB_HEADS:(u + 1) * SB_HEADS, :]
    a = jnp.exp2(log_beta + tail + jnp.concatenate(carries, axis=0))
    a_wide = jnp.concatenate([a[u * SB_HEADS:(u + 1) * SB_HEADS, :] for u in range(pps)], axis=1)
    acc = acc_ref[...] + _dot(a_wide.astype(bf16), head_major(v_refs))
    acc_ref[...] = acc
    carry_ref[...] = jnp.broadcast_to(carry, carry_ref.shape)

    @pl.when(j == pl.num_programs(1) - 1)
    def _():
        sub = lax.broadcasted_iota(jnp.int32, (SB_HEADS, BRANCH_W), 0)
        lane_head = lax.broadcasted_iota(jnp.int32, (SB_HEADS, BRANCH_W), 1) // SB_DIM
        o_ref[...] = jnp.sum(jnp.where(sub == lane_head, acc, 0.0), axis=0, keepdims=True)


def _sb_decode(q, bias_col, cache_k, cache_v, page_table, layer):
    b, n_pages = page_table.shape
    pps = DECODE_PAGES_PER_STEP
    while n_pages % pps:
        pps //= 2
    rows = PAGE * SB_HEADS

    def page_spec(u):
        return pl.BlockSpec((None, None, rows, SB_DIM),
                            lambda s, j, pt: (layer, pt[s, n_pages - 1 - (j * pps + u)], 0, 0))

    grid_spec = pltpu.PrefetchScalarGridSpec(
        num_scalar_prefetch=1,
        grid=(b, n_pages // pps),
        in_specs=[pl.BlockSpec((None, 1, BRANCH_W), lambda s, j, pt: (s, 0, 0)),
                  pl.BlockSpec((pps * SB_HEADS, 1), lambda s, j, pt: (0, 0))]
                 + [page_spec(u) for u in range(pps)] * 2,
        out_specs=pl.BlockSpec((None, 1, BRANCH_W), lambda s, j, pt: (s, 0, 0)),
        scratch_shapes=[pltpu.VMEM((SB_HEADS, BRANCH_W), f32), pltpu.VMEM((SB_HEADS, PAGE), f32),
                        pltpu.VMEM((BRANCH_W, SB_DIM), bf16)],
    )
    return pl.pallas_call(
        functools.partial(_sb_decode_kernel, pps=pps),
        out_shape=jax.ShapeDtypeStruct((b, 1, BRANCH_W), f32),
        grid_spec=grid_spec,
        compiler_params=_params("arbitrary", "arbitrary"),
        name="sb_decode",
    )(page_table, q, jnp.tile(bias_col, (pps, 1)), *([cache_k] * pps), *([cache_v] * pps))


def _gla_log_decay(ga, wa_ref, ba_ref):
    x = _dot(ga.astype(bf16), wa_ref[...]) + ba_ref[...]
    return (jnp.minimum(x, 0.0) - _softplus_neg_abs(x)) * (1.0 / GLA_TAU)


def _gla_out_norm(o, g, r):
    parts = []
    for h in range(GLA_HEADS):
        cols = slice(h * GLA_DV, (h + 1) * GLA_DV)
        parts.append(_rms(o[:, cols], g[:, cols]))
    return jnp.concatenate(parts, axis=1) * _silu(r)


def _gla_prompt_kernel(q_ref, k_ref, v_ref, r_ref, ga_ref, wa_ref, ba_ref, g_ref, o_ref, s_ref,
                       st_ref, b_ref, qe_ref, ke_ref, eb_ref, qs_ref, oacc_ref):
    t = q_ref.shape[0]
    sub = GLA_SUB
    step = pl.program_id(0)

    @pl.when(step == 0)
    def _():
        st_ref[...] = jnp.zeros_like(st_ref)

    log_a = _gla_log_decay(ga_ref[...], wa_ref, ba_ref)
    r = lax.broadcasted_iota(jnp.int32, (t, t), 0)
    c = lax.broadcasted_iota(jnp.int32, (t, t), 1)
    same = (r // sub) == (c // sub)
    incl = (same & (c <= r)).astype(bf16)
    whole = same.astype(bf16)
    hi, lo = _split_bf16(log_a)
    b = _dot(incl, hi) + _dot(incl, lo)
    b_end = _dot(whole, hi) + _dot(whole, lo)
    qs = q_ref[...] * (GLA_DK ** -0.5)
    b_ref[...] = b
    qs_ref[...] = qs
    qe_ref[...] = (qs * jnp.exp(b)).astype(bf16)
    ke_ref[...] = (k_ref[...] * jnp.exp(b_end - b)).astype(bf16)
    eb_ref[...] = jnp.exp(b_end)

    t_idx = lax.broadcasted_iota(jnp.int32, (sub, GLA_DK), 0)

    def body(n, _):
        rows = pl.ds(pl.multiple_of(n * sub, sub), sub)
        for h in range(GLA_HEADS):
            kc = pl.ds(h * GLA_DK, GLA_DK)
            vc = pl.ds(h * GLA_DV, GLA_DV)
            st = st_ref[h]
            o = lax.dot_general(qe_ref[rows, kc], st.astype(bf16), NT_DIMS, preferred_element_type=f32)
            bi = b_ref[rows, kc]
            qi = qs_ref[rows, kc]
            ki = k_ref[rows, kc]
            vi = v_ref[rows, vc]
            for s in range(sub):
                d = bi - bi[s:s + 1, :]
                e = jnp.exp(jnp.where(t_idx >= s, d, -jnp.inf))
                sc = jnp.sum(qi * e * ki[s:s + 1, :], axis=-1, keepdims=True)
                o = o + sc * vi[s:s + 1, :]
            oacc_ref[rows, vc] = o
            upd = lax.dot_general(vi.astype(bf16), ke_ref[rows, kc], TN_DIMS, preferred_element_type=f32)
            st_ref[h] = st * eb_ref[pl.ds(pl.multiple_of(n * sub, sub), 1), kc] + upd
        return 0

    lax.fori_loop(0, t // sub, body, 0)
    o_ref[...] = _gla_out_norm(oacc_ref[...], g_ref[...], r_ref[...]).astype(bf16)

    @pl.when(step == pl.num_programs(0) - 1)
    def _():
        for h in range(GLA_HEADS):
            s_ref[h] = st_ref[h].T


def _gla_prompt(p, pa, wa, ba, g):
    m = p.shape[0]
    t = min(GLA_BLOCK, m)
    return pl.pallas_call(
        _gla_prompt_kernel,
        out_shape=(jax.ShapeDtypeStruct((m, BRANCH_W), bf16),
                   jax.ShapeDtypeStruct((GLA_HEADS, GLA_DK, GLA_DV), f32)),
        grid=(m // t,),
        in_specs=[pl.BlockSpec((t, GLA_KEY), lambda i: (i, C_GQ // GLA_KEY)),
                  pl.BlockSpec((t, GLA_KEY), lambda i: (i, C_GK // GLA_KEY)),
                  pl.BlockSpec((t, BRANCH_W), lambda i: (i, C_GV // BRANCH_W)),
                  pl.BlockSpec((t, BRANCH_W), lambda i: (i, C_GR // BRANCH_W)),
                  pl.BlockSpec((t, GLA_RANK), lambda i: (i, 0)),
                  pl.BlockSpec((GLA_RANK, GLA_KEY), lambda i: (0, 0)),
                  pl.BlockSpec((1, GLA_KEY), lambda i: (0, 0)),
                  pl.BlockSpec((1, BRANCH_W), lambda i: (0, 0))],
        out_specs=(pl.BlockSpec((t, BRANCH_W), lambda i: (i, 0)),
                   pl.BlockSpec((GLA_HEADS, GLA_DK, GLA_DV), lambda i: (0, 0, 0))),
        scratch_shapes=[pltpu.VMEM((GLA_HEADS, GLA_DV, GLA_DK), f32),
                        pltpu.VMEM((t, GLA_KEY), f32),
                        pltpu.VMEM((t, GLA_KEY), bf16),
                        pltpu.VMEM((t, GLA_KEY), bf16),
                        pltpu.VMEM((t, GLA_KEY), f32),
                        pltpu.VMEM((t, GLA_KEY), f32),
                        pltpu.VMEM((t, BRANCH_W), f32)],
        compiler_params=_params("arbitrary"),
        name="gla_prompt",
    )(p, p, p, p, pa, wa, ba, g)


def _gla_sample_kernel(p_ref, ga_ref, s0_ref, wa_ref, ba_ref, g_ref, o_ref, s_ref):
    log_a = _gla_log_decay(ga_ref[...], wa_ref, ba_ref)
    decay = jnp.exp(log_a)
    q = p_ref[:, C_GQ:C_GQ + GLA_KEY] * (GLA_DK ** -0.5)
    k = p_ref[:, C_GK:C_GK + GLA_KEY]
    v = p_ref[:, C_GV:C_GV + BRANCH_W]
    row = lax.broadcasted_iota(jnp.int32, (8, GLA_DK), 0)
    parts = []
    for h in range(GLA_HEADS):
        kc = slice(h * GLA_DK, (h + 1) * GLA_DK)
        vc = slice(h * GLA_DV, (h + 1) * GLA_DV)
        s0 = s0_ref[h]
        score = jnp.sum(q[:, kc] * k[:, kc], axis=-1, keepdims=True)
        o = score * v[:, vc] + _dot((q[:, kc] * decay[:, kc]).astype(bf16), s0.astype(bf16))
        parts.append(o)
        tile = jnp.where(row == 0, k[:, kc], jnp.where(row == 1, decay[:, kc], 0.0))
        cols = tile.T
        s_ref[h] = cols[:, 1:2] * s0 + cols[:, 0:1] * v[:, vc]
    o_ref[...] = _gla_out_norm(jnp.concatenate(parts, axis=1), g_ref[...],
                               p_ref[:, C_GR:C_GR + BRANCH_W]).astype(bf16)


def _gla_sample(p3, pa3, state, layer, wa, ba, g):
    b = p3.shape[0]
    return pl.pallas_call(
        _gla_sample_kernel,
        out_shape=(jax.ShapeDtypeStruct((b, 1, BRANCH_W), bf16),
                   jax.ShapeDtypeStruct((b, GLA_HEADS, GLA_DK, GLA_DV), f32)),
        grid=(b,),
        in_specs=[pl.BlockSpec((None, 1, N_MAIN), lambda i: (i, 0, 0)),
                  pl.BlockSpec((None, 1, GLA_RANK), lambda i: (i, 0, 0)),
                  pl.BlockSpec((None, None, GLA_HEADS, GLA_DK, GLA_DV), lambda i: (layer, i, 0, 0, 0)),
                  pl.BlockSpec((GLA_RANK, GLA_KEY), lambda i: (0, 0)),
                  pl.BlockSpec((1, GLA_KEY), lambda i: (0, 0)),
                  pl.BlockSpec((1, BRANCH_W), lambda i: (0, 0))],
        out_specs=(pl.BlockSpec((None, 1, BRANCH_W), lambda i: (i, 0, 0)),
                   pl.BlockSpec((None, GLA_HEADS, GLA_DK, GLA_DV), lambda i: (i, 0, 0, 0))),
        compiler_params=_params("parallel"),
        name="gla_sample",
    )(p3, pa3, state, wa, ba, g)


def _merge_kernel(oa_ref, ob_ref, oc_ref, w_ref, ga_ref, gb_ref, gc_ref, o_ref):
    y = _sigmoid(ga_ref[...]) * _dot(oa_ref[...], w_ref[0])
    y = y + _sigmoid(gb_ref[...]) * _dot(ob_ref[...], w_ref[1])
    y = y + _sigmoid(gc_ref[...]) * _dot(oc_ref[...], w_ref[2])
    o_ref[...] = y.astype(bf16)


def _merge(oa, ob, oc, w_branch, layer, p, tm, tn):
    m = oa.shape[0]
    nj = D_MODEL // tn
    branch = pl.BlockSpec((tm, BRANCH_W), lambda i, j: (i, 0))

    def gate(n):
        return pl.BlockSpec((tm, tn), lambda i, j: (i, (C_GATE + n * D_MODEL) // tn + j))

    return pl.pallas_call(
        _merge_kernel,
        out_shape=jax.ShapeDtypeStruct((m, D_MODEL), bf16),
        grid=(m // tm, nj),
        in_specs=[branch, branch, branch,
                  pl.BlockSpec((None, N_BRANCH, BRANCH_W, tn), lambda i, j: (layer, 0, 0, j)),
                  gate(0), gate(1), gate(2)],
        out_specs=pl.BlockSpec((tm, tn), lambda i, j: (i, j)),
        compiler_params=_params("parallel", "arbitrary"),
        name="merge",
    )(oa, ob, oc, w_branch, p, p, p)


def _matmul_residual_kernel(a_ref, w_ref, x_ref, o_ref):
    o_ref[...] = x_ref[...] + _dot(a_ref[...], w_ref[...])


def _matmul_residual(a, w, layer, x, tm, tn):
    m, k = a.shape
    n = w.shape[2]
    return pl.pallas_call(
        _matmul_residual_kernel,
        out_shape=jax.ShapeDtypeStruct((m, n), f32),
        grid=(m // tm, n // tn),
        in_specs=[pl.BlockSpec((tm, k), lambda i, j: (i, 0)),
                  pl.BlockSpec((None, k, tn), lambda i, j: (layer, 0, j)),
                  pl.BlockSpec((tm, tn), lambda i, j: (i, j))],
        out_specs=pl.BlockSpec((tm, tn), lambda i, j: (i, j)),
        compiler_params=_params("parallel", "arbitrary"),
        name="matmul_residual",
    )(a, w, x)


def _ffn_in_kernel(x_ref, g_ref, wg_ref, wu_ref, o_ref, h_ref):
    @pl.when(pl.program_id(1) == 0)
    def _():
        h_ref[...] = _rms(x_ref[...], g_ref[...]).astype(bf16)

    h = h_ref[...]
    o_ref[...] = (_silu(_dot(h, wg_ref[...])) * _dot(h, wu_ref[...])).astype(bf16)


def _ffn_in(x, g, w, layer, tm, tn):
    m, d = x.shape
    nj = D_FF // tn
    return pl.pallas_call(
        _ffn_in_kernel,
        out_shape=jax.ShapeDtypeStruct((m, D_FF), bf16),
        grid=(m // tm, nj),
        in_specs=[pl.BlockSpec((tm, d), lambda i, j: (i, 0)),
                  pl.BlockSpec((1, d), lambda i, j: (0, 0)),
                  pl.BlockSpec((None, d, tn), lambda i, j: (layer, 0, j)),
                  pl.BlockSpec((None, d, tn), lambda i, j: (layer, 0, nj + j))],
        out_specs=pl.BlockSpec((tm, tn), lambda i, j: (i, j)),
        scratch_shapes=[pltpu.VMEM((tm, d), bf16)],
        compiler_params=_params("parallel", "arbitrary"),
        name="ffn_in",
    )(x, g, w, w)


def _kv_rows_kernel(*refs, depth):
    p_refs, (ko_ref, vo_ref) = refs[:2 * depth], refs[2 * depth:]
    layer = pl.program_id(0)
    rows = p_refs[0].shape[0]
    for l in range(depth):
        @pl.when(layer == l)
        def _():
            for src, dst in ((p_refs[2 * l], ko_ref), (p_refs[2 * l + 1], vo_ref)):
                for h in range(SB_HEADS):
                    dst[pl.ds(h, rows, stride=SB_HEADS), :] = src[:, h * SB_DIM:(h + 1) * SB_DIM]


def _kv_rows(ps, tm):
    depth = len(ps)
    m = ps[0].shape[0]
    nb = m // tm

    def src(l, col):
        return pl.BlockSpec((tm, BRANCH_W), lambda d, i: (jnp.where(d == l, i, jnp.where(d < l, 0, nb - 1)),
                                                          col // BRANCH_W))

    out = jax.ShapeDtypeStruct((depth, m * SB_HEADS, SB_DIM), f32)
    dst = pl.BlockSpec((None, tm * SB_HEADS, SB_DIM), lambda d, i: (d, i, 0))
    k, v = pl.pallas_call(
        functools.partial(_kv_rows_kernel, depth=depth),
        out_shape=(out, out),
        grid=(depth, nb),
        in_specs=[src(l, col) for l in range(depth) for col in (C_SK, C_SV)],
        out_specs=(dst, dst),
        compiler_params=_params("arbitrary", "arbitrary"),
        name="kv_rows",
    )(*[p for p in ps for _ in range(2)])
    shape = (depth, 1, m, SB_HEADS, SB_DIM)
    return k.reshape(shape), v.reshape(shape)


def _final_norm_kernel(x_ref, g_ref, o_ref):
    o_ref[...] = _rms(x_ref[...], g_ref[...])


def _final_norm(x, g, tm):
    m, d = x.shape
    return pl.pallas_call(
        _final_norm_kernel,
        out_shape=jax.ShapeDtypeStruct((m, d), f32),
        grid=(m // tm,),
        in_specs=[pl.BlockSpec((tm, d), lambda i: (i, 0)), pl.BlockSpec((1, d), lambda i: (0, 0))],
        out_specs=pl.BlockSpec((tm, d), lambda i: (i, 0)),
        compiler_params=_params("parallel"),
        name="final_norm",
    )(x, g)


def _matmul_weights(w_in, w_branch, w_o, w_ffn_in, w_ffn_out):
    return dict(
        w_in_t=jnp.swapaxes(w_in, 1, 2).astype(bf16),
        w_branch=w_branch.astype(bf16),
        w_o=w_o.astype(bf16),
        w_ffn_in=w_ffn_in.astype(bf16),
        w_ffn_out=w_ffn_out.astype(bf16),
    )


def _layer_params(l, g_mix, sgu_gain, w_spatial, b_spatial, sb_bias, w_gla_a2, b_gla_a, g_gla_out, g_ffn):
    return dict(
        g_mix=g_mix[l][None, :],
        sgu_gain=sgu_gain[l][None, :],
        ws_tril=jnp.tril(w_spatial[l]).astype(bf16),
        bs_full=jnp.repeat(b_spatial[l].T, GROUP_W, axis=1),
        ws_first=jnp.repeat(w_spatial[l][:, 0, 0], GROUP_W)[None, :],
        bs_first=jnp.repeat(b_spatial[l][:, 0], GROUP_W)[None, :],
        sb_bias=sb_bias[l],
        sb_bias_col=sb_bias[l][:, None],
        w_a2=w_gla_a2[l].astype(bf16),
        b_a=b_gla_a[l][None, :],
        g_gla=g_gla_out[l][None, :],
        g_ffn=g_ffn[l][None, :],
    )


def _tile(m, want):
    return want if m % want == 0 else m


def _in_projection(x, w, mw, l):
    return _in_projection_call(x, w["g_mix"], mw["w_in_t"], l, C_GATE, GLA_RANK, _tile(x.shape[0], 1024), 1024)


def _channel_mix(x, oa, ob, oc, p, w, mw, l):
    tm = _tile(x.shape[0], 1024)
    merged = _merge(oa, ob, oc, mw["w_branch"], l, p, tm, 512)
    x = _matmul_residual(merged, mw["w_o"], l, x, tm, 1024)
    f = _ffn_in(x, w["g_ffn"], mw["w_ffn_in"], l, tm, 512)
    return _matmul_residual(f, mw["w_ffn_out"], l, x, tm, 512)


def kernel(x_prompt, x_sample, cache_k, cache_v, state_gla, page_table, g_mix, w_in, sgu_gain, w_spatial, b_spatial, sb_bias, w_gla_a2, b_gla_a, g_gla_out, w_branch, w_o, g_ffn, w_ffn_in, w_ffn_out, g_final):
    depth = w_in.shape[0]
    batch, seq, _ = x_prompt.shape
    dec_batch, dec_seq, _ = x_sample.shape
    assert batch == 1 and dec_seq == 1
    assert cache_k.shape[2:] == (PAGE, SB_HEADS, SB_DIM)
    n_phys = cache_k.shape[1]
    ck = cache_k.reshape(depth, n_phys, PAGE * SB_HEADS, SB_DIM)
    cv = cache_v.reshape(depth, n_phys, PAGE * SB_HEADS, SB_DIM)

    xp = x_prompt.reshape(seq, D_MODEL)
    xs = x_sample.reshape(dec_batch, D_MODEL)
    tms = dec_batch
    mw = _matmul_weights(w_in, w_branch, w_o, w_ffn_in, w_ffn_out)
    p_p, s_p, p_s, s_s, cv_s = [], [], [], [], []
    for l in range(depth):
        w = _layer_params(l, g_mix, sgu_gain, w_spatial, b_spatial, sb_bias, w_gla_a2, b_gla_a, g_gla_out, g_ffn)
        p, pa = _in_projection(xp, w, mw, l)
        oa = _sgu_prompt(p, w["sgu_gain"], w["ws_tril"], w["bs_full"], _tile(seq, 256))
        ob = _sb_prompt(p, w["sb_bias"])
        oc, s_new = _gla_prompt(p, pa, w["w_a2"], w["b_a"], w["g_gla"])
        xp = _channel_mix(xp, oa, ob, oc, p, w, mw, l)
        p_p.append(p)
        s_p.append(s_new[None])
        p, pa = _in_projection(xs, w, mw, l)
        oa, sgu_v = _sgu_sample(p, w["sgu_gain"], w["ws_first"], w["bs_first"])
        q = p[:, C_SQ:C_SQ + BRANCH_W].reshape(dec_batch, 1, BRANCH_W)
        ob = _sb_decode(q, w["sb_bias_col"], ck, cv, page_table, l).reshape(dec_batch, BRANCH_W).astype(bf16)
        oc, s_new = _gla_sample(p.reshape(dec_batch, 1, N_MAIN), pa.reshape(dec_batch, 1, GLA_RANK), state_gla, l,
                                w["w_a2"], w["b_a"], w["g_gla"])
        xs = _channel_mix(xs, oa, ob, oc.reshape(dec_batch, BRANCH_W), p, w, mw, l)
        p_s.append(p)
        s_s.append(s_new)
        cv_s.append(sgu_v.reshape(dec_batch, 1, BRANCH_W))

    g_fin = g_final[None, :]
    y_prompt = _final_norm(xp, g_fin, _tile(seq, 512)).reshape(1, seq, D_MODEL)
    y_sample = _final_norm(xs, g_fin, tms).reshape(dec_batch, 1, D_MODEL)
    k_prompt, v_prompt = _kv_rows(p_p, _tile(seq, 512))
    k_sample, v_sample = (a.reshape(depth, dec_batch, 1, SB_HEADS, SB_DIM) for a in _kv_rows(p_s, dec_batch))
    return (y_prompt, y_sample, k_prompt, v_prompt, k_sample, v_sample,
            jnp.stack(s_p), jnp.stack(s_s), jnp.stack(cv_s))
```

```python
import functools

import jax
import jax.numpy as jnp
from jax import lax
from jax.experimental import pallas as pl
from jax.experimental.pallas import tpu as pltpu

f32 = jnp.float32
bf16 = jnp.bfloat16

D_MODEL = 2048
BRANCH_W = 1024
GROUP_W = 128
N_GROUPS = 8
SB_HEADS = 8
SB_DIM = 128
GLA_HEADS = 4
GLA_DK = 128
GLA_DV = 256
GLA_KEY = GLA_HEADS * GLA_DK
GLA_RANK = 16
GLA_TAU = 16.0
N_BRANCH = 3
D_FF = 5632
EPS = 1e-6
PAGE = 128
BF16_SUBLANES = 16
LOG2_E = 1.4426950408889634

C_AU, C_AV, C_SQ, C_SK, C_SV = 0, 1024, 2048, 3072, 4096
C_GQ, C_GK, C_GV, C_GR = 5120, 5632, 6144, 7168
C_GATE = 8192
N_MAIN = C_GATE + N_BRANCH * D_MODEL

V7X_VMEM_LIMIT = 56 * 1024 * 1024

SB_Q_TILE = 512
SB_K_TILE = 256
SB_ROW_CHUNK = 64
SB_HEADS_PER_STEP = 2
GLA_BLOCK = 256
GLA_SUB = 16
DECODE_PAGES_PER_STEP = 16

NT_DIMS = (((1,), (1,)), ((), ()))
TN_DIMS = (((0,), (0,)), ((), ()))


def _params(*sem):
    return pltpu.CompilerParams(dimension_semantics=sem, vmem_limit_bytes=V7X_VMEM_LIMIT)


def _dot(a, b):
    return jnp.dot(a, b, preferred_element_type=f32)


def _gelu(x):
    return 0.5 * x * (1.0 + lax.erf(x * 0.7071067811865476))


def _sigmoid(x):
    return 1.0 / (1.0 + jnp.exp(-x))


def _silu(x):
    return x * _sigmoid(x)


def _softplus_neg_abs(z):
    return jnp.log1p(jnp.exp(-jnp.abs(z)))


def _split_bf16(x):
    hi = x.astype(bf16)
    lo = (x - hi.astype(f32)).astype(bf16)
    return hi, lo


def _rms(x, g):
    ms = jnp.mean(x * x, axis=-1, keepdims=True)
    return x * lax.rsqrt(ms + EPS) * g


def _in_projection_kernel(x_ref, g_ref, w_ref, ws_ref, *refs, n_cast):
    src_refs = refs[:n_cast]
    o_ref, os_ref = refs[n_cast:n_cast + 2]
    dst_refs = refs[n_cast + 2:2 * n_cast + 2]
    h_ref = refs[2 * n_cast + 2]

    @pl.when(pl.program_id(1) == 0)
    def _():
        h = _rms(x_ref[...], g_ref[...]).astype(bf16)
        h_ref[...] = h
        os_ref[...] = lax.dot_general(h, ws_ref[...], NT_DIMS, preferred_element_type=f32)

    o_ref[...] = lax.dot_general(h_ref[...], w_ref[...], NT_DIMS, preferred_element_type=f32)
    for src, dst in zip(src_refs, dst_refs):
        dst[...] = src[...].astype(bf16)


def _in_projection_call(x, g, wt, layer, gap_row, gap, tm, tn, casts=()):
    m, d = x.shape
    n_out = wt.shape[1] - gap
    gap_block = gap_row // tn
    ni, nj = m // tm, n_out // tn

    def w_rows(i, j):
        return layer, pl.multiple_of(j * tn + jnp.where(j >= gap_block, gap, 0), BF16_SUBLANES), 0

    cast_in, cast_out, cast_shapes = [], [], []
    for stack, cast_layer in casts:
        rows, cols = stack.shape[1:]
        r = BF16_SUBLANES * pl.cdiv(rows, BF16_SUBLANES * ni * nj)
        last = pl.cdiv(rows, r) - 1

        def block(i, j, cast_layer=cast_layer, last=last):
            return cast_layer, jnp.minimum(i * nj + j, last), 0

        cast_in.append(pl.BlockSpec((None, r, cols), block))
        cast_out.append(pl.BlockSpec((None, r, cols), lambda i, j, last=last: (0, jnp.minimum(i * nj + j, last), 0)))
        cast_shapes.append(jax.ShapeDtypeStruct((1, rows, cols), bf16))

    outs = pl.pallas_call(
        functools.partial(_in_projection_kernel, n_cast=len(casts)),
        out_shape=(jax.ShapeDtypeStruct((m, n_out), f32), jax.ShapeDtypeStruct((m, gap), f32), *cast_shapes),
        grid=(ni, nj),
        in_specs=[pl.BlockSpec((tm, d), lambda i, j: (i, 0)),
                  pl.BlockSpec((1, d), lambda i, j: (0, 0)),
                  pl.BlockSpec((None, pl.Element(tn), pl.Element(d)), w_rows),
                  pl.BlockSpec((None, pl.Element(gap), pl.Element(d)), lambda i, j: (layer, gap_row, 0)),
                  *cast_in],
        out_specs=(pl.BlockSpec((tm, tn), lambda i, j: (i, j)), pl.BlockSpec((tm, gap), lambda i, j: (i, 0)),
                   *cast_out),
        scratch_shapes=[pltpu.VMEM((tm, d), bf16)],
        compiler_params=_params("arbitrary", "arbitrary"),
        name="in_projection",
    )(x, g, wt, wt, *[stack for stack, _ in casts])
    return outs[0], outs[1], list(outs[2:])


def _sgu_norm(av, gain):
    v = _gelu(av)
    v = v - jnp.mean(v, axis=-1, keepdims=True)
    v = v * lax.rsqrt(jnp.mean(v * v, axis=-1, keepdims=True) + EPS)
    return v * gain


def _sgu_prompt_kernel(au_ref, av_ref, gain_ref, ws_ref, bs_ref, o_ref, sv_ref):
    rows = au_ref.shape[0]
    sv_ref[...] = _sgu_norm(av_ref[...], gain_ref[...]).astype(bf16)
    for c in range(rows // GROUP_W):
        r = pl.ds(c * GROUP_W, GROUP_W)
        for g in range(N_GROUPS):
            cols = pl.ds(g * GROUP_W, GROUP_W)
            s = _dot(ws_ref[g], sv_ref[r, cols]) + bs_ref[:, cols]
            o_ref[r, cols] = (_gelu(au_ref[r, cols]) * s).astype(bf16)


def _sgu_prompt(p, gain, ws_tril, bs_full, rows):
    m = p.shape[0]
    return pl.pallas_call(
        _sgu_prompt_kernel,
        out_shape=jax.ShapeDtypeStruct((m, BRANCH_W), bf16),
        grid=(m // rows,),
        in_specs=[pl.BlockSpec((rows, BRANCH_W), lambda i: (i, C_AU // BRANCH_W)),
                  pl.BlockSpec((rows, BRANCH_W), lambda i: (i, C_AV // BRANCH_W)),
                  pl.BlockSpec((1, BRANCH_W), lambda i: (0, 0)),
                  pl.BlockSpec((N_GROUPS, GROUP_W, GROUP_W), lambda i: (0, 0, 0)),
                  pl.BlockSpec((GROUP_W, BRANCH_W), lambda i: (0, 0))],
        out_specs=pl.BlockSpec((rows, BRANCH_W), lambda i: (i, 0)),
        scratch_shapes=[pltpu.VMEM((rows, BRANCH_W), bf16)],
        compiler_params=_params("parallel"),
        name="sgu_prompt",
    )(p, p, gain, ws_tril, bs_full)


def _sgu_sample_kernel(au_ref, av_ref, gain_ref, w0_ref, b0_ref, o_ref, sv_ref):
    sv = _sgu_norm(av_ref[...], gain_ref[...])
    sv_ref[...] = sv
    o_ref[...] = (_gelu(au_ref[...]) * (w0_ref[...] * sv + b0_ref[...])).astype(bf16)


def _sgu_sample(p, gain, w0, b0):
    m = p.shape[0]
    row = pl.BlockSpec((1, BRANCH_W), lambda i: (0, 0))
    return pl.pallas_call(
        _sgu_sample_kernel,
        out_shape=(jax.ShapeDtypeStruct((m, BRANCH_W), bf16), jax.ShapeDtypeStruct((m, BRANCH_W), f32)),
        grid=(1,),
        in_specs=[pl.BlockSpec((m, BRANCH_W), lambda i: (0, C_AU // BRANCH_W)),
                  pl.BlockSpec((m, BRANCH_W), lambda i: (0, C_AV // BRANCH_W)),
                  row, row, row],
        out_specs=(pl.BlockSpec((m, BRANCH_W), lambda i: (0, 0)), pl.BlockSpec((m, BRANCH_W), lambda i: (0, 0))),
        compiler_params=_params("arbitrary"),
        name="sgu_sample",
    )(p, p, gain, w0, b0)


def _stick_logs(z2, causal=None):
    sp = jnp.log2(1.0 + jnp.exp2(-jnp.abs(z2)))
    log_beta = jnp.minimum(z2, 0.0) - sp
    log_rest = log_beta - z2
    if causal is not None:
        log_rest = jnp.where(causal, log_rest, 0.0)
    return log_beta, log_rest


def _local_tail(log_rest, upper):
    return _dot(log_rest.astype(bf16), upper)


def _strict_upper(t):
    r = lax.broadcasted_iota(jnp.int32, (t, t), 0)
    c = lax.broadcasted_iota(jnp.int32, (t, t), 1)
    return (r > c).astype(bf16)


def _sb_prompt_kernel(bias_ref, q_ref, k_ref, v_ref, o_ref, kb_ref, vb_ref, lb_ref, tail_ref, col_ref, *, tk):
    tq = q_ref.shape[0]
    hp = q_ref.shape[1] // SB_DIM
    g = pl.program_id(0)
    i = pl.program_id(1)

    @pl.when(i == 0)
    def _():
        kb_ref[...] = k_ref[...].astype(bf16)
        vb_ref[...] = v_ref[...].astype(bf16)

    scale = SB_DIM ** -0.5 * LOG2_E
    upper = _strict_upper(tk)
    q_pos = i * tq + lax.broadcasted_iota(jnp.int32, (tq, tk), 0)
    k_off = lax.broadcasted_iota(jnp.int32, (tq, tk), 1)
    heads = [pl.ds(h * SB_DIM, SB_DIM) for h in range(hp)]
    q = [q_ref[:, cols].astype(bf16) for cols in heads]
    bias = [bias_ref[g * hp + h] * LOG2_E for h in range(hp)]

    chunks = [slice(r, r + SB_ROW_CHUNK) for r in range(0, tq, SB_ROW_CHUNK)]

    def key_rows(j):
        return pl.ds(pl.multiple_of(j * tk, tk), tk)

    def logits(j):
        return [lax.dot_general(q[h], kb_ref[key_rows(j), heads[h]], NT_DIMS, preferred_element_type=f32)
                for h in range(hp)]

    def finish_stage_a(z, j, masked):
        causal = (j * tk + k_off < q_pos) if masked else None
        rest_bf = []
        for h in range(hp):
            parts = []
            for c in chunks:
                log_beta, log_rest = _stick_logs(z[h][c] * scale + bias[h], None if causal is None else causal[c])
                if masked:
                    log_beta = jnp.where(causal[c], log_beta, -jnp.inf)
                lb_ref[h, c, :] = log_beta
                col_ref[h, c, :] = log_rest[:, 0:1]
                parts.append(log_rest.astype(bf16))
            rest_bf.append(jnp.concatenate(parts, axis=0))
        for h in range(hp):
            tail_ref[h] = _dot(rest_bf[h], upper)

    def stage_b(state, j):
        out = []
        for h in range(hp):
            carry, acc = state[h]
            a = [jnp.exp2(lb_ref[h, c, :] + tail_ref[h, c, :] + carry[c]).astype(bf16) for c in chunks]
            carry = carry + tail_ref[h, :, 0:1] + col_ref[h]
            out.append((carry, acc + _dot(jnp.concatenate(a, axis=0), vb_ref[key_rows(j), heads[h]])))
        return tuple(out)

    def step(j_next, masked, state, j):
        z = logits(j_next)
        state = stage_b(state, j)
        finish_stage_a(z, j_next, masked)
        return state

    state = tuple((jnp.zeros((tq, 1), f32), jnp.zeros((tq, SB_DIM), f32)) for _ in range(hp))
    n_diag = tq // tk
    first = i * n_diag + n_diag - 1
    finish_stage_a(logits(first), first, True)
    for d in range(1, n_diag):
        state = step(first - d, True, state, first - d + 1)
    state = lax.fori_loop(0, i * n_diag, lambda n, s: step(i * n_diag - 1 - n, False, s, i * n_diag - n), state)
    state = stage_b(state, 0)
    for h in range(hp):
        o_ref[:, heads[h]] = state[h][1].astype(bf16)


def _sb_prompt(p, bias):
    m = p.shape[0]
    tq = min(SB_Q_TILE, m)
    tk = min(SB_K_TILE, m)
    w = SB_HEADS_PER_STEP * SB_DIM
    return pl.pallas_call(
        functools.partial(_sb_prompt_kernel, tk=tk),
        out_shape=jax.ShapeDtypeStruct((m, BRANCH_W), bf16),
        grid=(BRANCH_W // w, m // tq),
        in_specs=[pl.BlockSpec(memory_space=pltpu.SMEM),
                  pl.BlockSpec((tq, w), lambda g, i: (i, C_SQ // w + g)),
                  pl.BlockSpec((m, w), lambda g, i: (0, C_SK // w + g)),
                  pl.BlockSpec((m, w), lambda g, i: (0, C_SV // w + g))],
        out_specs=pl.BlockSpec((tq, w), lambda g, i: (i, g)),
        scratch_shapes=[pltpu.VMEM((m, w), bf16), pltpu.VMEM((m, w), bf16),
                        pltpu.VMEM((SB_HEADS_PER_STEP, tq, tk), f32), pltpu.VMEM((SB_HEADS_PER_STEP, tq, tk), f32),
                        pltpu.VMEM((SB_HEADS_PER_STEP, tq, 1), f32)],
        compiler_params=_params("arbitrary", "arbitrary"),
        name="sb_prompt",
    )(bias, p, p, p)


def _sb_decode_kernel(pt_ref, q_ref, bias_ref, *refs, pps):
    k_refs, v_refs = refs[:pps], refs[pps:2 * pps]
    o_ref, acc_ref, carry_ref, qc_ref = refs[2 * pps:]
    j = pl.program_id(1)

    @pl.when(j == 0)
    def _():
        acc_ref[...] = jnp.zeros_like(acc_ref)
        carry_ref[...] = jnp.zeros_like(carry_ref)
        row = lax.broadcasted_iota(jnp.int32, (PAGE, BRANCH_W), 0)
        lane_head = lax.broadcasted_iota(jnp.int32, (PAGE, BRANCH_W), 1) // SB_DIM
        q_rows = jnp.where(row == lane_head, jnp.broadcast_to(q_ref[...], (PAGE, BRANCH_W)), 0.0)
        qc_ref[...] = q_rows.T.astype(bf16)

    scale = SB_DIM ** -0.5 * LOG2_E
    upper = _strict_upper(PAGE)

    def head_major(refs_):
        return jnp.concatenate(
            [jnp.concatenate([ref[pl.ds(h, PAGE, stride=SB_HEADS), :] for h in range(SB_HEADS)], axis=1)
             for ref in refs_], axis=0).astype(bf16)

    zt = _dot(head_major(k_refs), qc_ref[...])
    z = jnp.concatenate([zt[u * PAGE:(u + 1) * PAGE, :].T[0:SB_HEADS, :] for u in range(pps)], axis=0)
    z = z * scale + bias_ref[...] * LOG2_E
    log_beta, log_rest = _stick_logs(z)
    tail = _local_tail(log_rest, upper)
    totals = tail[:, 0:1] + log_rest[:, 0:1]
    carry = carry_ref[:, 0:1]
    carries = []
    for u in range(pps):
        carries.append(carry)
        carry = carry + totals[u * SB_HEADS:(u + 1) * SB_HEADS, :]
    a = jnp.exp2(log_beta + tail + jnp.concatenate(carries, axis=0))
    a_wide = jnp.concatenate([a[u * SB_HEADS:(u + 1) * SB_HEADS, :] for u in range(pps)], axis=1)
    acc = acc_ref[...] + _dot(a_wide.astype(bf16), head_major(v_refs))
    acc_ref[...] = acc
    carry_ref[...] = jnp.broadcast_to(carry, carry_ref.shape)

    @pl.when(j == pl.num_programs(1) - 1)
    def _():
        sub = lax.broadcasted_iota(jnp.int32, (SB_HEADS, BRANCH_W), 0)
        lane_head = lax.broadcasted_iota(jnp.int32, (SB_HEADS, BRANCH_W), 1) // SB_DIM
        o_ref[...] = jnp.sum(jnp.where(sub == lane_head, acc, 0.0), axis=0, keepdims=True)


def _sb_decode(q, bias_col, cache_k, cache_v, page_table, layer):
    b, n_pages = page_table.shape
    pps = DECODE_PAGES_PER_STEP
    while n_pages % pps:
        pps //= 2
    rows = PAGE * SB_HEADS

    def page_spec(u):
        return pl.BlockSpec((None, None, rows, SB_DIM),
                            lambda s, j, pt: (layer, pt[s, n_pages - 1 - (j * pps + u)], 0, 0))

    grid_spec = pltpu.PrefetchScalarGridSpec(
        num_scalar_prefetch=1,
        grid=(b, n_pages // pps),
        in_specs=[pl.BlockSpec((None, 1, BRANCH_W), lambda s, j, pt: (s, 0, 0)),
                  pl.BlockSpec((pps * SB_HEADS, 1), lambda s, j, pt: (0, 0))]
                 + [page_spec(u) for u in range(pps)] * 2,
        out_specs=pl.BlockSpec((None, 1, BRANCH_W), lambda s, j, pt: (s, 0, 0)),
        scratch_shapes=[pltpu.VMEM((SB_HEADS, BRANCH_W), f32), pltpu.VMEM((SB_HEADS, PAGE), f32),
                        pltpu.VMEM((BRANCH_W, SB_DIM), bf16)],
    )
    return pl.pallas_call(
        functools.partial(_sb_decode_kernel, pps=pps),
        out_shape=jax.ShapeDtypeStruct((b, 1, BRANCH_W), f32),
        grid_spec=grid_spec,
        compiler_params=_params("arbitrary", "arbitrary"),
        name="sb_decode",
    )(page_table, q, jnp.tile(bias_col, (pps, 1)), *([cache_k] * pps), *([cache_v] * pps))


def _gla_log_decay(ga, wa_ref, ba_ref):
    x = _dot(ga.astype(bf16), wa_ref[...]) + ba_ref[...]
    return (jnp.minimum(x, 0.0) - _softplus_neg_abs(x)) * (1.0 / GLA_TAU)


def _gla_out_norm(o, g, r):
    parts = []
    for h in range(GLA_HEADS):
        cols = slice(h * GLA_DV, (h + 1) * GLA_DV)
        parts.append(_rms(o[:, cols], g[:, cols]))
    return jnp.concatenate(parts, axis=1) * _silu(r)


def _gla_prompt_kernel(q_ref, k_ref, v_ref, r_ref, ga_ref, wa_ref, ba_ref, g_ref, o_ref, s_ref,
                       st_ref, b_ref, qe_ref, ke_ref, eb_ref, qs_ref, oacc_ref):
    t = q_ref.shape[0]
    sub = GLA_SUB
    step = pl.program_id(0)

    @pl.when(step == 0)
    def _():
        st_ref[...] = jnp.zeros_like(st_ref)

    log_a = _gla_log_decay(ga_ref[...], wa_ref, ba_ref)
    r = lax.broadcasted_iota(jnp.int32, (t, t), 0)
    c = lax.broadcasted_iota(jnp.int32, (t, t), 1)
    same = (r // sub) == (c // sub)
    incl = (same & (c <= r)).astype(bf16)
    whole = same.astype(bf16)
    hi, lo = _split_bf16(log_a)
    b = _dot(incl, hi) + _dot(incl, lo)
    b_end = _dot(whole, hi) + _dot(whole, lo)
    qs = q_ref[...] * (GLA_DK ** -0.5)
    b_ref[...] = b
    qs_ref[...] = qs
    qe_ref[...] = (qs * jnp.exp(b)).astype(bf16)
    ke_ref[...] = (k_ref[...] * jnp.exp(b_end - b)).astype(bf16)
    eb_ref[...] = jnp.exp(b_end)

    t_idx = lax.broadcasted_iota(jnp.int32, (sub, GLA_DK), 0)

    def body(n, _):
        rows = pl.ds(pl.multiple_of(n * sub, sub), sub)
        for h in range(GLA_HEADS):
            kc = pl.ds(h * GLA_DK, GLA_DK)
            vc = pl.ds(h * GLA_DV, GLA_DV)
            st = st_ref[h]
            o = lax.dot_general(qe_ref[rows, kc], st.astype(bf16), NT_DIMS, preferred_element_type=f32)
            bi = b_ref[rows, kc]
            qi = qs_ref[rows, kc]
            ki = k_ref[rows, kc]
            vi = v_ref[rows, vc]
            for s in range(sub):
                d = bi - bi[s:s + 1, :]
                e = jnp.exp(jnp.where(t_idx >= s, d, -jnp.inf))
                sc = jnp.sum(qi * e * ki[s:s + 1, :], axis=-1, keepdims=True)
                o = o + sc * vi[s:s + 1, :]
            oacc_ref[rows, vc] = o
            upd = lax.dot_general(vi.astype(bf16), ke_ref[rows, kc], TN_DIMS, preferred_element_type=f32)
            st_ref[h] = st * eb_ref[pl.ds(pl.multiple_of(n * sub, sub), 1), kc] + upd
        return 0

    lax.fori_loop(0, t // sub, body, 0)
    o_ref[...] = _gla_out_norm(oacc_ref[...], g_ref[...], r_ref[...]).astype(bf16)

    @pl.when(step == pl.num_programs(0) - 1)
    def _():
        for h in range(GLA_HEADS):
            s_ref[h] = st_ref[h].T


def _gla_prompt(p, pa, wa, ba, g):
    m = p.shape[0]
    t = min(GLA_BLOCK, m)
    return pl.pallas_call(
        _gla_prompt_kernel,
        out_shape=(jax.ShapeDtypeStruct((m, BRANCH_W), bf16),
                   jax.ShapeDtypeStruct((GLA_HEADS, GLA_DK, GLA_DV), f32)),
        grid=(m // t,),
        in_specs=[pl.BlockSpec((t, GLA_KEY), lambda i: (i, C_GQ // GLA_KEY)),
                  pl.BlockSpec((t, GLA_KEY), lambda i: (i, C_GK // GLA_KEY)),
                  pl.BlockSpec((t, BRANCH_W), lambda i: (i, C_GV // BRANCH_W)),
                  pl.BlockSpec((t, BRANCH_W), lambda i: (i, C_GR // BRANCH_W)),
                  pl.BlockSpec((t, GLA_RANK), lambda i: (i, 0)),
                  pl.BlockSpec((GLA_RANK, GLA_KEY), lambda i: (0, 0)),
                  pl.BlockSpec((1, GLA_KEY), lambda i: (0, 0)),
                  pl.BlockSpec((1, BRANCH_W), lambda i: (0, 0))],
        out_specs=(pl.BlockSpec((t, BRANCH_W), lambda i: (i, 0)),
                   pl.BlockSpec((GLA_HEADS, GLA_DK, GLA_DV), lambda i: (0, 0, 0))),
        scratch_shapes=[pltpu.VMEM((GLA_HEADS, GLA_DV, GLA_DK), f32),
                        pltpu.VMEM((t, GLA_KEY), f32),
                        pltpu.VMEM((t, GLA_KEY), bf16),
                        pltpu.VMEM((t, GLA_KEY), bf16),
                        pltpu.VMEM((t, GLA_KEY), f32),
                        pltpu.VMEM((t, GLA_KEY), f32),
                        pltpu.VMEM((t, BRANCH_W), f32)],
        compiler_params=_params("arbitrary"),
        name="gla_prompt",
    )(p, p, p, p, pa, wa, ba, g)


def _gla_sample_kernel(p_ref, ga_ref, s0_ref, wa_ref, ba_ref, g_ref, o_ref, s_ref):
    log_a = _gla_log_decay(ga_ref[...], wa_ref, ba_ref)
    decay = jnp.exp(log_a)
    q = p_ref[:, C_GQ:C_GQ + GLA_KEY] * (GLA_DK ** -0.5)
    k = p_ref[:, C_GK:C_GK + GLA_KEY]
    v = p_ref[:, C_GV:C_GV + BRANCH_W]
    row = lax.broadcasted_iota(jnp.int32, (8, GLA_DK), 0)
    parts = []
    for h in range(GLA_HEADS):
        kc = slice(h * GLA_DK, (h + 1) * GLA_DK)
        vc = slice(h * GLA_DV, (h + 1) * GLA_DV)
        s0 = s0_ref[h]
        score = jnp.sum(q[:, kc] * k[:, kc], axis=-1, keepdims=True)
        o = score * v[:, vc] + _dot((q[:, kc] * decay[:, kc]).astype(bf16), s0.astype(bf16))
        parts.append(o)
        tile = jnp.where(row == 0, k[:, kc], jnp.where(row == 1, decay[:, kc], 0.0))
        cols = tile.T
        s_ref[h] = cols[:, 1:2] * s0 + cols[:, 0:1] * v[:, vc]
    o_ref[...] = _gla_out_norm(jnp.concatenate(parts, axis=1), g_ref[...],
                               p_ref[:, C_GR:C_GR + BRANCH_W]).astype(bf16)


def _gla_sample(p3, pa3, state, layer, wa, ba, g):
    b = p3.shape[0]
    return pl.pallas_call(
        _gla_sample_kernel,
        out_shape=(jax.ShapeDtypeStruct((b, 1, BRANCH_W), bf16),
                   jax.ShapeDtypeStruct((b, GLA_HEADS, GLA_DK, GLA_DV), f32)),
        grid=(b,),
        in_specs=[pl.BlockSpec((None, 1, N_MAIN), lambda i: (i, 0, 0)),
                  pl.BlockSpec((None, 1, GLA_RANK), lambda i: (i, 0, 0)),
                  pl.BlockSpec((None, None, GLA_HEADS, GLA_DK, GLA_DV), lambda i: (layer, i, 0, 0, 0)),
                  pl.BlockSpec((GLA_RANK, GLA_KEY), lambda i: (0, 0)),
                  pl.BlockSpec((1, GLA_KEY), lambda i: (0, 0)),
                  pl.BlockSpec((1, BRANCH_W), lambda i: (0, 0))],
        out_specs=(pl.BlockSpec((None, 1, BRANCH_W), lambda i: (i, 0, 0)),
                   pl.BlockSpec((None, GLA_HEADS, GLA_DK, GLA_DV), lambda i: (i, 0, 0, 0))),
        compiler_params=_params("parallel"),
        name="gla_sample",
    )(p3, pa3, state, wa, ba, g)


def _merge_kernel(oa_ref, ob_ref, oc_ref, w_ref, ga_ref, gb_ref, gc_ref, o_ref):
    y = _sigmoid(ga_ref[...]) * _dot(oa_ref[...], w_ref[0])
    y = y + _sigmoid(gb_ref[...]) * _dot(ob_ref[...], w_ref[1])
    y = y + _sigmoid(gc_ref[...]) * _dot(oc_ref[...], w_ref[2])
    o_ref[...] = y.astype(bf16)


def _merge(oa, ob, oc, w_branch, layer, p, tm, tn):
    m = oa.shape[0]
    nj = D_MODEL // tn
    branch = pl.BlockSpec((tm, BRANCH_W), lambda i, j: (i, 0))

    def gate(n):
        return pl.BlockSpec((tm, tn), lambda i, j: (i, (C_GATE + n * D_MODEL) // tn + j))

    return pl.pallas_call(
        _merge_kernel,
        out_shape=jax.ShapeDtypeStruct((m, D_MODEL), bf16),
        grid=(m // tm, nj),
        in_specs=[branch, branch, branch,
                  pl.BlockSpec((None, N_BRANCH, BRANCH_W, tn), lambda i, j: (layer, 0, 0, j)),
                  gate(0), gate(1), gate(2)],
        out_specs=pl.BlockSpec((tm, tn), lambda i, j: (i, j)),
        compiler_params=_params("parallel", "arbitrary"),
        name="merge",
    )(oa, ob, oc, w_branch, p, p, p)


def _matmul_residual_kernel(a_ref, w_ref, x_ref, o_ref):
    o_ref[...] = x_ref[...] + _dot(a_ref[...], w_ref[...])


def _matmul_residual(a, w, layer, x, tm, tn):
    m, k = a.shape
    n = w.shape[2]
    return pl.pallas_call(
        _matmul_residual_kernel,
        out_shape=jax.ShapeDtypeStruct((m, n), f32),
        grid=(m // tm, n // tn),
        in_specs=[pl.BlockSpec((tm, k), lambda i, j: (i, 0)),
                  pl.BlockSpec((None, k, tn), lambda i, j: (layer, 0, j)),
                  pl.BlockSpec((tm, tn), lambda i, j: (i, j))],
        out_specs=pl.BlockSpec((tm, tn), lambda i, j: (i, j)),
        compiler_params=_params("parallel", "arbitrary"),
        name="matmul_residual",
    )(a, w, x)


def _ffn_in_kernel(x_ref, g_ref, wg_ref, wu_ref, o_ref, h_ref):
    @pl.when(pl.program_id(1) == 0)
    def _():
        h_ref[...] = _rms(x_ref[...], g_ref[...]).astype(bf16)

    h = h_ref[...]
    o_ref[...] = (_silu(_dot(h, wg_ref[...])) * _dot(h, wu_ref[...])).astype(bf16)


def _ffn_in(x, g, w, layer, tm, tn):
    m, d = x.shape
    nj = D_FF // tn
    return pl.pallas_call(
        _ffn_in_kernel,
        out_shape=jax.ShapeDtypeStruct((m, D_FF), bf16),
        grid=(m // tm, nj),
        in_specs=[pl.BlockSpec((tm, d), lambda i, j: (i, 0)),
                  pl.BlockSpec((1, d), lambda i, j: (0, 0)),
                  pl.BlockSpec((None, d, tn), lambda i, j: (layer, 0, j)),
                  pl.BlockSpec((None, d, tn), lambda i, j: (layer, 0, nj + j))],
        out_specs=pl.BlockSpec((tm, tn), lambda i, j: (i, j)),
        scratch_shapes=[pltpu.VMEM((tm, d), bf16)],
        compiler_params=_params("parallel", "arbitrary"),
        name="ffn_in",
    )(x, g, w, w)


def _kv_rows_kernel(*refs, depth):
    p_refs, (ko_ref, vo_ref) = refs[:2 * depth], refs[2 * depth:]
    layer = pl.program_id(0)
    rows = p_refs[0].shape[0]
    for l in range(depth):
        @pl.when(layer == l)
        def _():
            for src, dst in ((p_refs[2 * l], ko_ref), (p_refs[2 * l + 1], vo_ref)):
                for h in range(SB_HEADS):
                    dst[pl.ds(h, rows, stride=SB_HEADS), :] = src[:, h * SB_DIM:(h + 1) * SB_DIM]


def _kv_rows(ps, tm):
    depth = len(ps)
    m = ps[0].shape[0]
    nb = m // tm

    def src(l, col):
        return pl.BlockSpec((tm, BRANCH_W), lambda d, i: (jnp.where(d == l, i, jnp.where(d < l, 0, nb - 1)),
                                                          col // BRANCH_W))

    out = jax.ShapeDtypeStruct((depth, m * SB_HEADS, SB_DIM), f32)
    dst = pl.BlockSpec((None, tm * SB_HEADS, SB_DIM), lambda d, i: (d, i, 0))
    k, v = pl.pallas_call(
        functools.partial(_kv_rows_kernel, depth=depth),
        out_shape=(out, out),
        grid=(depth, nb),
        in_specs=[src(l, col) for l in range(depth) for col in (C_SK, C_SV)],
        out_specs=(dst, dst),
        compiler_params=_params("arbitrary", "arbitrary"),
        name="kv_rows",
    )(*[p for p in ps for _ in range(2)])
    shape = (depth, 1, m, SB_HEADS, SB_DIM)
    return k.reshape(shape), v.reshape(shape)


def _final_norm_kernel(x_ref, g_ref, o_ref):
    o_ref[...] = _rms(x_ref[...], g_ref[...])


def _final_norm(x, g, tm):
    m, d = x.shape
    return pl.pallas_call(
        _final_norm_kernel,
        out_shape=jax.ShapeDtypeStruct((m, d), f32),
        grid=(m // tm,),
        in_specs=[pl.BlockSpec((tm, d), lambda i: (i, 0)), pl.BlockSpec((1, d), lambda i: (0, 0))],
        out_specs=pl.BlockSpec((tm, d), lambda i: (i, 0)),
        compiler_params=_params("parallel"),
        name="final_norm",
    )(x, g)


CAST_ORDER = ("w_branch", "w_o", "w_ffn_in", "w_ffn_out")


def _weight_stacks(w_in, w_branch, w_o, w_ffn_in, w_ffn_out):
    depth = w_in.shape[0]
    return dict(
        w_in_t=jnp.swapaxes(w_in, 1, 2),
        w_branch=w_branch.reshape(depth, N_BRANCH * BRANCH_W, D_MODEL),
        w_o=w_o,
        w_ffn_in=w_ffn_in,
        w_ffn_out=w_ffn_out,
    )


def _layer_params(l, g_mix, sgu_gain, w_spatial, b_spatial, sb_bias, w_gla_a2, b_gla_a, g_gla_out, g_ffn):
    return dict(
        g_mix=g_mix[l][None, :],
        sgu_gain=sgu_gain[l][None, :],
        ws_tril=jnp.tril(w_spatial[l]).astype(bf16),
        bs_full=jnp.repeat(b_spatial[l].T, GROUP_W, axis=1),
        ws_first=jnp.repeat(w_spatial[l][:, 0, 0], GROUP_W)[None, :],
        bs_first=jnp.repeat(b_spatial[l][:, 0], GROUP_W)[None, :],
        sb_bias=sb_bias[l],
        sb_bias_col=sb_bias[l][:, None],
        w_a2=w_gla_a2[l].astype(bf16),
        b_a=b_gla_a[l][None, :],
        g_gla=g_gla_out[l][None, :],
        g_ffn=g_ffn[l][None, :],
    )


def _tile(m, want):
    return want if m % want == 0 else m


def _in_projection(x, w, w_in_bf, casts=()):
    return _in_projection_call(x, w["g_mix"], w_in_bf, 0, C_GATE, GLA_RANK, _tile(x.shape[0], 1024), 1024, casts)


def _channel_mix(x, oa, ob, oc, p, w, mw):
    tm = _tile(x.shape[0], 1024)
    merged = _merge(oa, ob, oc, mw["w_branch"].reshape(1, N_BRANCH, BRANCH_W, D_MODEL), 0, p, tm, 512)
    x = _matmul_residual(merged, mw["w_o"], 0, x, tm, 1024)
    f = _ffn_in(x, w["g_ffn"], mw["w_ffn_in"], 0, tm, 512)
    return _matmul_residual(f, mw["w_ffn_out"], 0, x, tm, 512)


def kernel(x_prompt, x_sample, cache_k, cache_v, state_gla, page_table, g_mix, w_in, sgu_gain, w_spatial, b_spatial, sb_bias, w_gla_a2, b_gla_a, g_gla_out, w_branch, w_o, g_ffn, w_ffn_in, w_ffn_out, g_final):
    depth = w_in.shape[0]
    batch, seq, _ = x_prompt.shape
    dec_batch, dec_seq, _ = x_sample.shape
    assert batch == 1 and dec_seq == 1
    assert cache_k.shape[2:] == (PAGE, SB_HEADS, SB_DIM)
    n_phys = cache_k.shape[1]
    ck = cache_k.reshape(depth, n_phys, PAGE * SB_HEADS, SB_DIM)
    cv = cache_v.reshape(depth, n_phys, PAGE * SB_HEADS, SB_DIM)

    xp = x_prompt.reshape(seq, D_MODEL)
    xs = x_sample.reshape(dec_batch, D_MODEL)
    tms = dec_batch
    stacks = _weight_stacks(w_in, w_branch, w_o, w_ffn_in, w_ffn_out)
    w_in_bf = stacks["w_in_t"][0:1].astype(bf16)
    p_p, s_p, p_s, s_s, cv_s = [], [], [], [], []
    for l in range(depth):
        w = _layer_params(l, g_mix, sgu_gain, w_spatial, b_spatial, sb_bias, w_gla_a2, b_gla_a, g_gla_out, g_ffn)
        casts = [(stacks[name], l) for name in CAST_ORDER] + ([(stacks["w_in_t"], l + 1)] if l + 1 < depth else [])
        p, pa, cast = _in_projection(xp, w, w_in_bf, casts)
        mw = dict(zip(CAST_ORDER, cast))
        oa = _sgu_prompt(p, w["sgu_gain"], w["ws_tril"], w["bs_full"], _tile(seq, 256))
        ob = _sb_prompt(p, w["sb_bias"])
        oc, s_new = _gla_prompt(p, pa, w["w_a2"], w["b_a"], w["g_gla"])
        xp = _channel_mix(xp, oa, ob, oc, p, w, mw)
        p_p.append(p)
        s_p.append(s_new[None])
        p, pa, _ = _in_projection(xs, w, w_in_bf)
        oa, sgu_v = _sgu_sample(p, w["sgu_gain"], w["ws_first"], w["bs_first"])
        q = p[:, C_SQ:C_SQ + BRANCH_W].reshape(dec_batch, 1, BRANCH_W)
        ob = _sb_decode(q, w["sb_bias_col"], ck, cv, page_table, l).reshape(dec_batch, BRANCH_W).astype(bf16)
        oc, s_new = _gla_sample(p.reshape(dec_batch, 1, N_MAIN), pa.reshape(dec_batch, 1, GLA_RANK), state_gla, l,
                                w["w_a2"], w["b_a"], w["g_gla"])
        xs = _channel_mix(xs, oa, ob, oc.reshape(dec_batch, BRANCH_W), p, w, mw)
        p_s.append(p)
        s_s.append(s_new)
        cv_s.append(sgu_v.reshape(dec_batch, 1, BRANCH_W))
        if l + 1 < depth:
            w_in_bf = cast[len(CAST_ORDER)]

    g_fin = g_final[None, :]
    y_prompt = _final_norm(xp, g_fin, _tile(seq, 512)).reshape(1, seq, D_MODEL)
    y_sample = _final_norm(xs, g_fin, tms).reshape(dec_batch, 1, D_MODEL)
    k_prompt, v_prompt = _kv_rows(p_p, _tile(seq, 512))
    k_sample, v_sample = (a.reshape(depth, dec_batch, 1, SB_HEADS, SB_DIM) for a in _kv_rows(p_s, dec_batch))
    return (y_prompt, y_sample, k_prompt, v_prompt, k_sample, v_sample,
            jnp.stack(s_p), jnp.stack(s_s), jnp.stack(cv_s))
```

```python
import functools

import jax
import jax.numpy as jnp
from jax import lax
from jax.experimental import pallas as pl
from jax.experimental.pallas import tpu as pltpu

f32 = jnp.float32
bf16 = jnp.bfloat16

D_MODEL = 2048
BRANCH_W = 1024
GROUP_W = 128
N_GROUPS = 8
SB_HEADS = 8
SB_DIM = 128
GLA_HEADS = 4
GLA_DK = 128
GLA_DV = 256
GLA_KEY = GLA_HEADS * GLA_DK
GLA_RANK = 16
GLA_TAU = 16.0
N_BRANCH = 3
D_FF = 5632
EPS = 1e-6
PAGE = 128
BF16_SUBLANES = 16
LOG2_E = 1.4426950408889634

C_AU, C_AV, C_SQ, C_SK, C_SV = 0, 1024, 2048, 3072, 4096
C_GQ, C_GK, C_GV, C_GR = 5120, 5632, 6144, 7168
C_GATE = 8192
N_MAIN = C_GATE + N_BRANCH * D_MODEL

V7X_VMEM_LIMIT = 52 * 1024 * 1024
V7X_VMEM_LIMIT_IN_PROJECTION = 56 * 1024 * 1024

SB_Q_TILE = 512
SB_K_TILE = 256
SB_ROW_CHUNK = 64
SB_HEADS_PER_STEP = 2
GLA_BLOCK = 256
GLA_SUB = 16
DECODE_PAGES_PER_STEP = 16

NT_DIMS = (((1,), (1,)), ((), ()))
TN_DIMS = (((0,), (0,)), ((), ()))


def _params(*sem, vmem=V7X_VMEM_LIMIT):
    return pltpu.CompilerParams(dimension_semantics=sem, vmem_limit_bytes=vmem)


def _dot(a, b):
    return jnp.dot(a, b, preferred_element_type=f32)


def _gelu(x):
    return 0.5 * x * (1.0 + lax.erf(x * 0.7071067811865476))


def _sigmoid(x):
    return 1.0 / (1.0 + jnp.exp(-x))


def _silu(x):
    return x * _sigmoid(x)


def _softplus_neg_abs(z):
    return jnp.log1p(jnp.exp(-jnp.abs(z)))


def _split_bf16(x):
    hi = x.astype(bf16)
    lo = (x - hi.astype(f32)).astype(bf16)
    return hi, lo


def _rms(x, g):
    ms = jnp.mean(x * x, axis=-1, keepdims=True)
    return x * lax.rsqrt(ms + EPS) * g


def _in_projection_kernel(x_ref, g_ref, w_ref, ws_ref, *refs, n_cast):
    src_refs = refs[:n_cast]
    o_ref, os_ref = refs[n_cast:n_cast + 2]
    dst_refs = refs[n_cast + 2:2 * n_cast + 2]
    h_ref = refs[2 * n_cast + 2]

    @pl.when(pl.program_id(1) == 0)
    def _():
        h = _rms(x_ref[...], g_ref[...]).astype(bf16)
        h_ref[...] = h
        os_ref[...] = lax.dot_general(h, ws_ref[...], NT_DIMS, preferred_element_type=f32)

    o_ref[...] = lax.dot_general(h_ref[...], w_ref[...], NT_DIMS, preferred_element_type=f32)
    for src, dst in zip(src_refs, dst_refs):
        dst[...] = src[...].astype(bf16)


def _in_projection_call(x, g, wt, layer, gap_row, gap, tm, tn, casts=()):
    m, d = x.shape
    n_out = wt.shape[1] - gap
    gap_block = gap_row // tn
    ni, nj = m // tm, n_out // tn

    def w_rows(i, j):
        return layer, pl.multiple_of(j * tn + jnp.where(j >= gap_block, gap, 0), BF16_SUBLANES), 0

    cast_in, cast_out, cast_shapes = [], [], []
    for stack, cast_layer in casts:
        rows, cols = stack.shape[1:]
        r = BF16_SUBLANES * pl.cdiv(rows, BF16_SUBLANES * ni * nj)
        last = pl.cdiv(rows, r) - 1

        def block(i, j, cast_layer=cast_layer, last=last):
            return cast_layer, jnp.minimum(i * nj + j, last), 0

        cast_in.append(pl.BlockSpec((None, r, cols), block))
        cast_out.append(pl.BlockSpec((None, r, cols), lambda i, j, last=last: (0, jnp.minimum(i * nj + j, last), 0)))
        cast_shapes.append(jax.ShapeDtypeStruct((1, rows, cols), bf16))

    outs = pl.pallas_call(
        functools.partial(_in_projection_kernel, n_cast=len(casts)),
        out_shape=(jax.ShapeDtypeStruct((m, n_out), f32), jax.ShapeDtypeStruct((m, gap), f32), *cast_shapes),
        grid=(ni, nj),
        in_specs=[pl.BlockSpec((tm, d), lambda i, j: (i, 0)),
                  pl.BlockSpec((1, d), lambda i, j: (0, 0)),
                  pl.BlockSpec((None, pl.Element(tn), pl.Element(d)), w_rows),
                  pl.BlockSpec((None, pl.Element(gap), pl.Element(d)), lambda i, j: (layer, gap_row, 0)),
                  *cast_in],
        out_specs=(pl.BlockSpec((tm, tn), lambda i, j: (i, j)), pl.BlockSpec((tm, gap), lambda i, j: (i, 0)),
                   *cast_out),
        scratch_shapes=[pltpu.VMEM((tm, d), bf16)],
        compiler_params=_params("arbitrary", "arbitrary", vmem=V7X_VMEM_LIMIT_IN_PROJECTION),
        name="in_projection",
    )(x, g, wt, wt, *[stack for stack, _ in casts])
    return outs[0], outs[1], list(outs[2:])


def _sgu_norm(av, gain):
    v = _gelu(av)
    v = v - jnp.mean(v, axis=-1, keepdims=True)
    v = v * lax.rsqrt(jnp.mean(v * v, axis=-1, keepdims=True) + EPS)
    return v * gain


def _sgu_prompt_kernel(au_ref, av_ref, gain_ref, ws_ref, bs_ref, o_ref, sv_ref):
    rows = au_ref.shape[0]
    sv_ref[...] = _sgu_norm(av_ref[...], gain_ref[...]).astype(bf16)
    for c in range(rows // GROUP_W):
        r = pl.ds(c * GROUP_W, GROUP_W)
        for g in range(N_GROUPS):
            cols = pl.ds(g * GROUP_W, GROUP_W)
            s = _dot(ws_ref[g], sv_ref[r, cols]) + bs_ref[:, cols]
            o_ref[r, cols] = (_gelu(au_ref[r, cols]) * s).astype(bf16)


def _sgu_prompt(p, gain, ws_tril, bs_full, rows):
    m = p.shape[0]
    return pl.pallas_call(
        _sgu_prompt_kernel,
        out_shape=jax.ShapeDtypeStruct((m, BRANCH_W), bf16),
        grid=(m // rows,),
        in_specs=[pl.BlockSpec((rows, BRANCH_W), lambda i: (i, C_AU // BRANCH_W)),
                  pl.BlockSpec((rows, BRANCH_W), lambda i: (i, C_AV // BRANCH_W)),
                  pl.BlockSpec((1, BRANCH_W), lambda i: (0, 0)),
                  pl.BlockSpec((N_GROUPS, GROUP_W, GROUP_W), lambda i: (0, 0, 0)),
                  pl.BlockSpec((GROUP_W, BRANCH_W), lambda i: (0, 0))],
        out_specs=pl.BlockSpec((rows, BRANCH_W), lambda i: (i, 0)),
        scratch_shapes=[pltpu.VMEM((rows, BRANCH_W), bf16)],
        compiler_params=_params("parallel"),
        name="sgu_prompt",
    )(p, p, gain, ws_tril, bs_full)


def _sgu_sample_kernel(au_ref, av_ref, gain_ref, w0_ref, b0_ref, o_ref, sv_ref):
    sv = _sgu_norm(av_ref[...], gain_ref[...])
    sv_ref[...] = sv
    o_ref[...] = (_gelu(au_ref[...]) * (w0_ref[...] * sv + b0_ref[...])).astype(bf16)


def _sgu_sample(p, gain, w0, b0):
    m = p.shape[0]
    row = pl.BlockSpec((1, BRANCH_W), lambda i: (0, 0))
    return pl.pallas_call(
        _sgu_sample_kernel,
        out_shape=(jax.ShapeDtypeStruct((m, BRANCH_W), bf16), jax.ShapeDtypeStruct((m, BRANCH_W), f32)),
        grid=(1,),
        in_specs=[pl.BlockSpec((m, BRANCH_W), lambda i: (0, C_AU // BRANCH_W)),
                  pl.BlockSpec((m, BRANCH_W), lambda i: (0, C_AV // BRANCH_W)),
                  row, row, row],
        out_specs=(pl.BlockSpec((m, BRANCH_W), lambda i: (0, 0)), pl.BlockSpec((m, BRANCH_W), lambda i: (0, 0))),
        compiler_params=_params("arbitrary"),
        name="sgu_sample",
    )(p, p, gain, w0, b0)


def _stick_logs(z2, causal=None):
    sp = jnp.log2(1.0 + jnp.exp2(-jnp.abs(z2)))
    log_beta = jnp.minimum(z2, 0.0) - sp
    log_rest = log_beta - z2
    if causal is not None:
        log_rest = jnp.where(causal, log_rest, 0.0)
    return log_beta, log_rest


def _local_tail(log_rest, upper):
    return _dot(log_rest.astype(bf16), upper)


def _strict_upper(t):
    r = lax.broadcasted_iota(jnp.int32, (t, t), 0)
    c = lax.broadcasted_iota(jnp.int32, (t, t), 1)
    return (r > c).astype(bf16)


def _sb_prompt_kernel(bias_ref, q_ref, k_ref, v_ref, o_ref, kb_ref, vb_ref, lb_ref, tail_ref, col_ref, *, tk):
    tq = q_ref.shape[0]
    hp = q_ref.shape[1] // SB_DIM
    g = pl.program_id(0)
    i = pl.program_id(1)

    @pl.when(i == 0)
    def _():
        kb_ref[...] = k_ref[...].astype(bf16)
        vb_ref[...] = v_ref[...].astype(bf16)

    scale = SB_DIM ** -0.5 * LOG2_E
    upper = _strict_upper(tk)
    q_pos = i * tq + lax.broadcasted_iota(jnp.int32, (tq, tk), 0)
    k_off = lax.broadcasted_iota(jnp.int32, (tq, tk), 1)
    heads = [pl.ds(h * SB_DIM, SB_DIM) for h in range(hp)]
    q = [q_ref[:, cols].astype(bf16) for cols in heads]
    bias = [bias_ref[g * hp + h] * LOG2_E for h in range(hp)]

    chunks = [slice(r, r + SB_ROW_CHUNK) for r in range(0, tq, SB_ROW_CHUNK)]

    def key_rows(j):
        return pl.ds(pl.multiple_of(j * tk, tk), tk)

    def logits(j):
        return [lax.dot_general(q[h], kb_ref[key_rows(j), heads[h]], NT_DIMS, preferred_element_type=f32)
                for h in range(hp)]

    def finish_stage_a(z, j, masked):
        causal = (j * tk + k_off < q_pos) if masked else None
        rest_bf = []
        for h in range(hp):
            parts = []
            for c in chunks:
                log_beta, log_rest = _stick_logs(z[h][c] * scale + bias[h], None if causal is None else causal[c])
                if masked:
                    log_beta = jnp.where(causal[c], log_beta, -jnp.inf)
                lb_ref[h, c, :] = log_beta
                col_ref[h, c, :] = log_rest[:, 0:1]
                parts.append(log_rest.astype(bf16))
            rest_bf.append(jnp.concatenate(parts, axis=0))
        for h in range(hp):
            tail_ref[h] = _dot(rest_bf[h], upper)

    def stage_b(state, j):
        out = []
        for h in range(hp):
            carry, acc = state[h]
            a = [jnp.exp2(lb_ref[h, c, :] + tail_ref[h, c, :] + carry[c]).astype(bf16) for c in chunks]
            carry = carry + tail_ref[h, :, 0:1] + col_ref[h]
            out.append((carry, acc + _dot(jnp.concatenate(a, axis=0), vb_ref[key_rows(j), heads[h]])))
        return tuple(out)

    def step(j_next, masked, state, j):
        z = logits(j_next)
        state = stage_b(state, j)
        finish_stage_a(z, j_next, masked)
        return state

    state = tuple((jnp.zeros((tq, 1), f32), jnp.zeros((tq, SB_DIM), f32)) for _ in range(hp))
    n_diag = tq // tk
    first = i * n_diag + n_diag - 1
    finish_stage_a(logits(first), first, True)
    for d in range(1, n_diag):
        state = step(first - d, True, state, first - d + 1)
    state = lax.fori_loop(0, i * n_diag, lambda n, s: step(i * n_diag - 1 - n, False, s, i * n_diag - n), state)
    state = stage_b(state, 0)
    for h in range(hp):
        o_ref[:, heads[h]] = state[h][1].astype(bf16)


def _sb_prompt(p, bias):
    m = p.shape[0]
    tq = min(SB_Q_TILE, m)
    tk = min(SB_K_TILE, m)
    w = SB_HEADS_PER_STEP * SB_DIM
    return pl.pallas_call(
        functools.partial(_sb_prompt_kernel, tk=tk),
        out_shape=jax.ShapeDtypeStruct((m, BRANCH_W), bf16),
        grid=(BRANCH_W // w, m // tq),
        in_specs=[pl.BlockSpec(memory_space=pltpu.SMEM),
                  pl.BlockSpec((tq, w), lambda g, i: (i, C_SQ // w + g)),
                  pl.BlockSpec((m, w), lambda g, i: (0, C_SK // w + g)),
                  pl.BlockSpec((m, w), lambda g, i: (0, C_SV // w + g))],
        out_specs=pl.BlockSpec((tq, w), lambda g, i: (i, g)),
        scratch_shapes=[pltpu.VMEM((m, w), bf16), pltpu.VMEM((m, w), bf16),
                        pltpu.VMEM((SB_HEADS_PER_STEP, tq, tk), f32), pltpu.VMEM((SB_HEADS_PER_STEP, tq, tk), f32),
                        pltpu.VMEM((SB_HEADS_PER_STEP, tq, 1), f32)],
        compiler_params=_params("arbitrary", "arbitrary"),
        name="sb_prompt",
    )(bias, p, p, p)


def _sb_decode_kernel(pt_ref, q_ref, bias_ref, *refs, pps):
    k_refs, v_refs = refs[:pps], refs[pps:2 * pps]
    o_ref, acc_ref, carry_ref, qc_ref = refs[2 * pps:]
    j = pl.program_id(1)

    @pl.when(j == 0)
    def _():
        acc_ref[...] = jnp.zeros_like(acc_ref)
        carry_ref[...] = jnp.zeros_like(carry_ref)
        row = lax.broadcasted_iota(jnp.int32, (PAGE, BRANCH_W), 0)
        lane_head = lax.broadcasted_iota(jnp.int32, (PAGE, BRANCH_W), 1) // SB_DIM
        q_rows = jnp.where(row == lane_head, jnp.broadcast_to(q_ref[...], (PAGE, BRANCH_W)), 0.0)
        qc_ref[...] = q_rows.T.astype(bf16)

    scale = SB_DIM ** -0.5 * LOG2_E
    upper = _strict_upper(PAGE)

    def head_major(refs_):
        return jnp.concatenate(
            [jnp.concatenate([ref[pl.ds(h, PAGE, stride=SB_HEADS), :] for h in range(SB_HEADS)], axis=1)
             for ref in refs_], axis=0).astype(bf16)

    zt = _dot(head_major(k_refs), qc_ref[...])
    z = jnp.concatenate([zt[u * PAGE:(u + 1) * PAGE, :].T[0:SB_HEADS, :] for u in range(pps)], axis=0)
    z = z * scale + bias_ref[...] * LOG2_E
    log_beta, log_rest = _stick_logs(z)
    tail = _local_tail(log_rest, upper)
    totals = tail[:, 0:1] + log_rest[:, 0:1]
    carry = carry_ref[:, 0:1]
    carries = []
    for u in range(pps):
        carries.append(carry)
        carry = carry + totals[u * SB_HEADS:(u + 1) * SB_HEADS, :]
    a = jnp.exp2(log_beta + tail + jnp.concatenate(carries, axis=0))
    a_wide = jnp.concatenate([a[u * SB_HEADS:(u + 1) * SB_HEADS, :] for u in range(pps)], axis=1)
    acc = acc_ref[...] + _dot(a_wide.astype(bf16), head_major(v_refs))
    acc_ref[...] = acc
    carry_ref[...] = jnp.broadcast_to(carry, carry_ref.shape)

    @pl.when(j == pl.num_programs(1) - 1)
    def _():
        sub = lax.broadcasted_iota(jnp.int32, (SB_HEADS, BRANCH_W), 0)
        lane_head = lax.broadcasted_iota(jnp.int32, (SB_HEADS, BRANCH_W), 1) // SB_DIM
        o_ref[...] = jnp.sum(jnp.where(sub == lane_head, acc, 0.0), axis=0, keepdims=True)


def _sb_decode(q, bias_col, cache_k, cache_v, page_table, layer):
    b, n_pages = page_table.shape
    pps = DECODE_PAGES_PER_STEP
    while n_pages % pps:
        pps //= 2
    rows = PAGE * SB_HEADS

    def page_spec(u):
        return pl.BlockSpec((None, None, rows, SB_DIM),
                            lambda s, j, pt: (layer, pt[s, n_pages - 1 - (j * pps + u)], 0, 0))

    grid_spec = pltpu.PrefetchScalarGridSpec(
        num_scalar_prefetch=1,
        grid=(b, n_pages // pps),
        in_specs=[pl.BlockSpec((None, 1, BRANCH_W), lambda s, j, pt: (s, 0, 0)),
                  pl.BlockSpec((pps * SB_HEADS, 1), lambda s, j, pt: (0, 0))]
                 + [page_spec(u) for u in range(pps)] * 2,
        out_specs=pl.BlockSpec((None, 1, BRANCH_W), lambda s, j, pt: (s, 0, 0)),
        scratch_shapes=[pltpu.VMEM((SB_HEADS, BRANCH_W), f32), pltpu.VMEM((SB_HEADS, PAGE), f32),
                        pltpu.VMEM((BRANCH_W, SB_DIM), bf16)],
    )
    return pl.pallas_call(
        functools.partial(_sb_decode_kernel, pps=pps),
        out_shape=jax.ShapeDtypeStruct((b, 1, BRANCH_W), f32),
        grid_spec=grid_spec,
        compiler_params=_params("arbitrary", "arbitrary"),
        name="sb_decode",
    )(page_table, q, jnp.tile(bias_col, (pps, 1)), *([cache_k] * pps), *([cache_v] * pps))


def _gla_log_decay(ga, wa_ref, ba_ref):
    x = _dot(ga.astype(bf16), wa_ref[...]) + ba_ref[...]
    return (jnp.minimum(x, 0.0) - _softplus_neg_abs(x)) * (1.0 / GLA_TAU)


def _gla_out_norm(o, g, r):
    parts = []
    for h in range(GLA_HEADS):
        cols = slice(h * GLA_DV, (h + 1) * GLA_DV)
        parts.append(_rms(o[:, cols], g[:, cols]))
    return jnp.concatenate(parts, axis=1) * _silu(r)


def _gla_prompt_kernel(q_ref, k_ref, v_ref, r_ref, ga_ref, wa_ref, ba_ref, g_ref, o_ref, s_ref,
                       st_ref, b_ref, qe_ref, ke_ref, eb_ref, qs_ref, oacc_ref):
    t = q_ref.shape[0]
    sub = GLA_SUB
    step = pl.program_id(0)

    @pl.when(step == 0)
    def _():
        st_ref[...] = jnp.zeros_like(st_ref)

    log_a = _gla_log_decay(ga_ref[...], wa_ref, ba_ref)
    r = lax.broadcasted_iota(jnp.int32, (t, t), 0)
    c = lax.broadcasted_iota(jnp.int32, (t, t), 1)
    same = (r // sub) == (c // sub)
    incl = (same & (c <= r)).astype(bf16)
    whole = same.astype(bf16)
    hi, lo = _split_bf16(log_a)
    b = (_dot(incl, hi) + _dot(incl, lo)) * LOG2_E
    b_end = (_dot(whole, hi) + _dot(whole, lo)) * LOG2_E
    qs = q_ref[...] * (GLA_DK ** -0.5)
    b_ref[...] = b
    qs_ref[...] = qs
    qe_ref[...] = (qs * jnp.exp2(b)).astype(bf16)
    ke_ref[...] = (k_ref[...] * jnp.exp2(b_end - b)).astype(bf16)
    eb_ref[...] = jnp.exp2(b_end)

    t_col = lax.broadcasted_iota(jnp.int32, (sub, 1), 0)

    def body(n, _):
        rows = pl.ds(pl.multiple_of(n * sub, sub), sub)
        for h in range(GLA_HEADS):
            kc = pl.ds(h * GLA_DK, GLA_DK)
            vc = pl.ds(h * GLA_DV, GLA_DV)
            st = st_ref[h]
            o = lax.dot_general(qe_ref[rows, kc], st.astype(bf16), NT_DIMS, preferred_element_type=f32)
            bi = b_ref[rows, kc]
            qi = qs_ref[rows, kc]
            ki = k_ref[rows, kc]
            vi = v_ref[rows, vc]
            for s in range(sub):
                e = jnp.exp2(jnp.minimum(bi - bi[s:s + 1, :], 0.0))
                sc = jnp.sum(qi * e * ki[s:s + 1, :], axis=-1, keepdims=True)
                o = o + jnp.where(t_col >= s, sc, 0.0) * vi[s:s + 1, :]
            oacc_ref[rows, vc] = o
            upd = lax.dot_general(vi.astype(bf16), ke_ref[rows, kc], TN_DIMS, preferred_element_type=f32)
            st_ref[h] = st * eb_ref[pl.ds(pl.multiple_of(n * sub, sub), 1), kc] + upd
        return 0

    lax.fori_loop(0, t // sub, body, 0)
    o_ref[...] = _gla_out_norm(oacc_ref[...], g_ref[...], r_ref[...]).astype(bf16)

    @pl.when(step == pl.num_programs(0) - 1)
    def _():
        for h in range(GLA_HEADS):
            s_ref[h] = st_ref[h].T


def _gla_prompt(p, pa, wa, ba, g):
    m = p.shape[0]
    t = min(GLA_BLOCK, m)
    return pl.pallas_call(
        _gla_prompt_kernel,
        out_shape=(jax.ShapeDtypeStruct((m, BRANCH_W), bf16),
                   jax.ShapeDtypeStruct((GLA_HEADS, GLA_DK, GLA_DV), f32)),
        grid=(m // t,),
        in_specs=[pl.BlockSpec((t, GLA_KEY), lambda i: (i, C_GQ // GLA_KEY)),
                  pl.BlockSpec((t, GLA_KEY), lambda i: (i, C_GK // GLA_KEY)),
                  pl.BlockSpec((t, BRANCH_W), lambda i: (i, C_GV // BRANCH_W)),
                  pl.BlockSpec((t, BRANCH_W), lambda i: (i, C_GR // BRANCH_W)),
                  pl.BlockSpec((t, GLA_RANK), lambda i: (i, 0)),
                  pl.BlockSpec((GLA_RANK, GLA_KEY), lambda i: (0, 0)),
                  pl.BlockSpec((1, GLA_KEY), lambda i: (0, 0)),
                  pl.BlockSpec((1, BRANCH_W), lambda i: (0, 0))],
        out_specs=(pl.BlockSpec((t, BRANCH_W), lambda i: (i, 0)),
                   pl.BlockSpec((GLA_HEADS, GLA_DK, GLA_DV), lambda i: (0, 0, 0))),
        scratch_shapes=[pltpu.VMEM((GLA_HEADS, GLA_DV, GLA_DK), f32),
                        pltpu.VMEM((t, GLA_KEY), f32),
                        pltpu.VMEM((t, GLA_KEY), bf16),
                        pltpu.VMEM((t, GLA_KEY), bf16),
                        pltpu.VMEM((t, GLA_KEY), f32),
                        pltpu.VMEM((t, GLA_KEY), f32),
                        pltpu.VMEM((t, BRANCH_W), f32)],
        compiler_params=_params("arbitrary"),
        name="gla_prompt",
    )(p, p, p, p, pa, wa, ba, g)


def _gla_sample_kernel(p_ref, ga_ref, s0_ref, wa_ref, ba_ref, g_ref, o_ref, s_ref):
    log_a = _gla_log_decay(ga_ref[...], wa_ref, ba_ref)
    decay = jnp.exp(log_a)
    q = p_ref[:, C_GQ:C_GQ + GLA_KEY] * (GLA_DK ** -0.5)
    k = p_ref[:, C_GK:C_GK + GLA_KEY]
    v = p_ref[:, C_GV:C_GV + BRANCH_W]
    row = lax.broadcasted_iota(jnp.int32, (8, GLA_DK), 0)
    parts = []
    for h in range(GLA_HEADS):
        kc = slice(h * GLA_DK, (h + 1) * GLA_DK)
        vc = slice(h * GLA_DV, (h + 1) * GLA_DV)
        s0 = s0_ref[h]
        score = jnp.sum(q[:, kc] * k[:, kc], axis=-1, keepdims=True)
        o = score * v[:, vc] + _dot((q[:, kc] * decay[:, kc]).astype(bf16), s0.astype(bf16))
        parts.append(o)
        tile = jnp.where(row == 0, k[:, kc], jnp.where(row == 1, decay[:, kc], 0.0))
        cols = tile.T
        s_ref[h] = cols[:, 1:2] * s0 + cols[:, 0:1] * v[:, vc]
    o_ref[...] = _gla_out_norm(jnp.concatenate(parts, axis=1), g_ref[...],
                               p_ref[:, C_GR:C_GR + BRANCH_W]).astype(bf16)


def _gla_sample(p3, pa3, state, layer, wa, ba, g):
    b = p3.shape[0]
    return pl.pallas_call(
        _gla_sample_kernel,
        out_shape=(jax.ShapeDtypeStruct((b, 1, BRANCH_W), bf16),
                   jax.ShapeDtypeStruct((b, GLA_HEADS, GLA_DK, GLA_DV), f32)),
        grid=(b,),
        in_specs=[pl.BlockSpec((None, 1, N_MAIN), lambda i: (i, 0, 0)),
                  pl.BlockSpec((None, 1, GLA_RANK), lambda i: (i, 0, 0)),
                  pl.BlockSpec((None, None, GLA_HEADS, GLA_DK, GLA_DV), lambda i: (layer, i, 0, 0, 0)),
                  pl.BlockSpec((GLA_RANK, GLA_KEY), lambda i: (0, 0)),
                  pl.BlockSpec((1, GLA_KEY), lambda i: (0, 0)),
                  pl.BlockSpec((1, BRANCH_W), lambda i: (0, 0))],
        out_specs=(pl.BlockSpec((None, 1, BRANCH_W), lambda i: (i, 0, 0)),
                   pl.BlockSpec((None, GLA_HEADS, GLA_DK, GLA_DV), lambda i: (i, 0, 0, 0))),
        compiler_params=_params("parallel"),
        name="gla_sample",
    )(p3, pa3, state, wa, ba, g)


def _merge_kernel(oa_ref, ob_ref, oc_ref, w_ref, ga_ref, gb_ref, gc_ref, o_ref):
    y = _sigmoid(ga_ref[...]) * _dot(oa_ref[...], w_ref[0])
    y = y + _sigmoid(gb_ref[...]) * _dot(ob_ref[...], w_ref[1])
    y = y + _sigmoid(gc_ref[...]) * _dot(oc_ref[...], w_ref[2])
    o_ref[...] = y.astype(bf16)


def _merge(oa, ob, oc, w_branch, layer, p, tm, tn):
    m = oa.shape[0]
    nj = D_MODEL // tn
    branch = pl.BlockSpec((tm, BRANCH_W), lambda i, j: (i, 0))

    def gate(n):
        return pl.BlockSpec((tm, tn), lambda i, j: (i, (C_GATE + n * D_MODEL) // tn + j))

    return pl.pallas_call(
        _merge_kernel,
        out_shape=jax.ShapeDtypeStruct((m, D_MODEL), bf16),
        grid=(m // tm, nj),
        in_specs=[branch, branch, branch,
                  pl.BlockSpec((None, N_BRANCH, BRANCH_W, tn), lambda i, j: (layer, 0, 0, j)),
                  gate(0), gate(1), gate(2)],
        out_specs=pl.BlockSpec((tm, tn), lambda i, j: (i, j)),
        compiler_params=_params("parallel", "arbitrary"),
        name="merge",
    )(oa, ob, oc, w_branch, p, p, p)


def _matmul_residual_kernel(a_ref, w_ref, x_ref, o_ref):
    o_ref[...] = x_ref[...] + _dot(a_ref[...], w_ref[...])


def _matmul_residual(a, w, layer, x, tm, tn):
    m, k = a.shape
    n = w.shape[2]
    return pl.pallas_call(
        _matmul_residual_kernel,
        out_shape=jax.ShapeDtypeStruct((m, n), f32),
        grid=(m // tm, n // tn),
        in_specs=[pl.BlockSpec((tm, k), lambda i, j: (i, 0)),
                  pl.BlockSpec((None, k, tn), lambda i, j: (layer, 0, j)),
                  pl.BlockSpec((tm, tn), lambda i, j: (i, j))],
        out_specs=pl.BlockSpec((tm, tn), lambda i, j: (i, j)),
        compiler_params=_params("parallel", "arbitrary"),
        name="matmul_residual",
    )(a, w, x)


def _ffn_in_kernel(x_ref, g_ref, wg_ref, wu_ref, o_ref, h_ref):
    @pl.when(pl.program_id(1) == 0)
    def _():
        h_ref[...] = _rms(x_ref[...], g_ref[...]).astype(bf16)

    h = h_ref[...]
    o_ref[...] = (_silu(_dot(h, wg_ref[...])) * _dot(h, wu_ref[...])).astype(bf16)


def _ffn_in(x, g, w, layer, tm, tn):
    m, d = x.shape
    nj = D_FF // tn
    return pl.pallas_call(
        _ffn_in_kernel,
        out_shape=jax.ShapeDtypeStruct((m, D_FF), bf16),
        grid=(m // tm, nj),
        in_specs=[pl.BlockSpec((tm, d), lambda i, j: (i, 0)),
                  pl.BlockSpec((1, d), lambda i, j: (0, 0)),
                  pl.BlockSpec((None, d, tn), lambda i, j: (layer, 0, j)),
                  pl.BlockSpec((None, d, tn), lambda i, j: (layer, 0, nj + j))],
        out_specs=pl.BlockSpec((tm, tn), lambda i, j: (i, j)),
        scratch_shapes=[pltpu.VMEM((tm, d), bf16)],
        compiler_params=_params("parallel", "arbitrary"),
        name="ffn_in",
    )(x, g, w, w)


def _kv_rows_kernel(*refs, depth):
    p_refs, (ko_ref, vo_ref) = refs[:2 * depth], refs[2 * depth:]
    layer = pl.program_id(0)
    rows = p_refs[0].shape[0]
    for l in range(depth):
        @pl.when(layer == l)
        def _():
            for src, dst in ((p_refs[2 * l], ko_ref), (p_refs[2 * l + 1], vo_ref)):
                for h in range(SB_HEADS):
                    dst[pl.ds(h, rows, stride=SB_HEADS), :] = src[:, h * SB_DIM:(h + 1) * SB_DIM]


def _kv_rows(ps, tm):
    depth = len(ps)
    m = ps[0].shape[0]
    nb = m // tm

    def src(l, col):
        return pl.BlockSpec((tm, BRANCH_W), lambda d, i: (jnp.where(d == l, i, jnp.where(d < l, 0, nb - 1)),
                                                          col // BRANCH_W))

    out = jax.ShapeDtypeStruct((depth, m * SB_HEADS, SB_DIM), f32)
    dst = pl.BlockSpec((None, tm * SB_HEADS, SB_DIM), lambda d, i: (d, i, 0))
    k, v = pl.pallas_call(
        functools.partial(_kv_rows_kernel, depth=depth),
        out_shape=(out, out),
        grid=(depth, nb),
        in_specs=[src(l, col) for l in range(depth) for col in (C_SK, C_SV)],
        out_specs=(dst, dst),
        compiler_params=_params("arbitrary", "arbitrary"),
        name="kv_rows",
    )(*[p for p in ps for _ in range(2)])
    shape = (depth, 1, m, SB_HEADS, SB_DIM)
    return k.reshape(shape), v.reshape(shape)


def _final_norm_kernel(x_ref, g_ref, o_ref):
    o_ref[...] = _rms(x_ref[...], g_ref[...])


def _final_norm(x, g, tm):
    m, d = x.shape
    return pl.pallas_call(
        _final_norm_kernel,
        out_shape=jax.ShapeDtypeStruct((m, d), f32),
        grid=(m // tm,),
        in_specs=[pl.BlockSpec((tm, d), lambda i: (i, 0)), pl.BlockSpec((1, d), lambda i: (0, 0))],
        out_specs=pl.BlockSpec((tm, d), lambda i: (i, 0)),
        compiler_params=_params("parallel"),
        name="final_norm",
    )(x, g)


CAST_ORDER = ("w_branch", "w_o", "w_ffn_in", "w_ffn_out")


def _weight_stacks(w_in, w_branch, w_o, w_ffn_in, w_ffn_out):
    depth = w_in.shape[0]
    return dict(
        w_in_t=jnp.swapaxes(w_in, 1, 2),
        w_branch=w_branch.reshape(depth, N_BRANCH * BRANCH_W, D_MODEL),
        w_o=w_o,
        w_ffn_in=w_ffn_in,
        w_ffn_out=w_ffn_out,
    )


def _layer_params(l, g_mix, sgu_gain, w_spatial, b_spatial, sb_bias, w_gla_a2, b_gla_a, g_gla_out, g_ffn):
    return dict(
        g_mix=g_mix[l][None, :],
        sgu_gain=sgu_gain[l][None, :],
        ws_tril=jnp.tril(w_spatial[l]).astype(bf16),
        bs_full=jnp.repeat(b_spatial[l].T, GROUP_W, axis=1),
        ws_first=jnp.repeat(w_spatial[l][:, 0, 0], GROUP_W)[None, :],
        bs_first=jnp.repeat(b_spatial[l][:, 0], GROUP_W)[None, :],
        sb_bias=sb_bias[l],
        sb_bias_col=sb_bias[l][:, None],
        w_a2=w_gla_a2[l].astype(bf16),
        b_a=b_gla_a[l][None, :],
        g_gla=g_gla_out[l][None, :],
        g_ffn=g_ffn[l][None, :],
    )


def _tile(m, want):
    return want if m % want == 0 else m


def _in_projection(x, w, w_in_bf, casts=()):
    tm = _tile(x.shape[0], 1024)
    tn = 1024 if tm == 1024 else 2048
    return _in_projection_call(x, w["g_mix"], w_in_bf, 0, C_GATE, GLA_RANK, tm, tn, casts)


def _channel_mix(x, oa, ob, oc, p, w, mw):
    tm = _tile(x.shape[0], 1024)
    t_merge, t_o, t_in, t_out = (512, 1024, 512, 512) if tm == 1024 else (1024, 2048, D_FF // 4, 1024)
    merged = _merge(oa, ob, oc, mw["w_branch"].reshape(1, N_BRANCH, BRANCH_W, D_MODEL), 0, p, tm, t_merge)
    x = _matmul_residual(merged, mw["w_o"], 0, x, tm, t_o)
    f = _ffn_in(x, w["g_ffn"], mw["w_ffn_in"], 0, tm, t_in)
    return _matmul_residual(f, mw["w_ffn_out"], 0, x, tm, t_out)


def kernel(x_prompt, x_sample, cache_k, cache_v, state_gla, page_table, g_mix, w_in, sgu_gain, w_spatial, b_spatial, sb_bias, w_gla_a2, b_gla_a, g_gla_out, w_branch, w_o, g_ffn, w_ffn_in, w_ffn_out, g_final):
    depth = w_in.shape[0]
    batch, seq, _ = x_prompt.shape
    dec_batch, dec_seq, _ = x_sample.shape
    assert batch == 1 and dec_seq == 1
    assert cache_k.shape[2:] == (PAGE, SB_HEADS, SB_DIM)
    n_phys = cache_k.shape[1]
    ck = cache_k.reshape(depth, n_phys, PAGE * SB_HEADS, SB_DIM)
    cv = cache_v.reshape(depth, n_phys, PAGE * SB_HEADS, SB_DIM)

    xp = x_prompt.reshape(seq, D_MODEL)
    xs = x_sample.reshape(dec_batch, D_MODEL)
    tms = dec_batch
    stacks = _weight_stacks(w_in, w_branch, w_o, w_ffn_in, w_ffn_out)
    w_in_bf = stacks["w_in_t"][0:1].astype(bf16)
    p_p, s_p, p_s, s_s, cv_s = [], [], [], [], []
    for l in range(depth):
        w = _layer_params(l, g_mix, sgu_gain, w_spatial, b_spatial, sb_bias, w_gla_a2, b_gla_a, g_gla_out, g_ffn)
        casts = [(stacks[name], l) for name in CAST_ORDER] + ([(stacks["w_in_t"], l + 1)] if l + 1 < depth else [])
        p, pa, cast = _in_projection(xp, w, w_in_bf, casts)
        mw = dict(zip(CAST_ORDER, cast))
        oa = _sgu_prompt(p, w["sgu_gain"], w["ws_tril"], w["bs_full"], _tile(seq, 256))
        ob = _sb_prompt(p, w["sb_bias"])
        oc, s_new = _gla_prompt(p, pa, w["w_a2"], w["b_a"], w["g_gla"])
        xp = _channel_mix(xp, oa, ob, oc, p, w, mw)
        p_p.append(p)
        s_p.append(s_new[None])
        p, pa, _ = _in_projection(xs, w, w_in_bf)
        oa, sgu_v = _sgu_sample(p, w["sgu_gain"], w["ws_first"], w["bs_first"])
        q = p[:, C_SQ:C_SQ + BRANCH_W].reshape(dec_batch, 1, BRANCH_W)
        ob = _sb_decode(q, w["sb_bias_col"], ck, cv, page_table, l).reshape(dec_batch, BRANCH_W).astype(bf16)
        oc, s_new = _gla_sample(p.reshape(dec_batch, 1, N_MAIN), pa.reshape(dec_batch, 1, GLA_RANK), state_gla, l,
                                w["w_a2"], w["b_a"], w["g_gla"])
        xs = _channel_mix(xs, oa, ob, oc.reshape(dec_batch, BRANCH_W), p, w, mw)
        p_s.append(p)
        s_s.append(s_new)
        cv_s.append(sgu_v.reshape(dec_batch, 1, BRANCH_W))
        if l + 1 < depth:
            w_in_bf = cast[len(CAST_ORDER)]

    g_fin = g_final[None, :]
    y_prompt = _final_norm(xp, g_fin, _tile(seq, 512)).reshape(1, seq, D_MODEL)
    y_sample = _final_norm(xs, g_fin, tms).reshape(dec_batch, 1, D_MODEL)
    k_prompt, v_prompt = _kv_rows(p_p, _tile(seq, 512))
    k_sample, v_sample = (a.reshape(depth, dec_batch, 1, SB_HEADS, SB_DIM) for a in _kv_rows(p_s, dec_batch))
    return (y_prompt, y_sample, k_prompt, v_prompt, k_sample, v_sample,
            jnp.stack(s_p), jnp.stack(s_s), jnp.stack(cv_s))
```

```python
import functools

import jax
import jax.numpy as jnp
from jax import lax
from jax.experimental import pallas as pl
from jax.experimental.pallas import tpu as pltpu

f32 = jnp.float32
bf16 = jnp.bfloat16

D_MODEL = 2048
BRANCH_W = 1024
GROUP_W = 128
N_GROUPS = 8
SB_HEADS = 8
SB_DIM = 128
GLA_HEADS = 4
GLA_DK = 128
GLA_DV = 256
GLA_KEY = GLA_HEADS * GLA_DK
GLA_RANK = 16
GLA_TAU = 16.0
N_BRANCH = 3
D_FF = 5632
EPS = 1e-6
PAGE = 128
BF16_SUBLANES = 16
LOG2_E = 1.4426950408889634

C_AU, C_AV, C_SQ, C_SK, C_SV = 0, 1024, 2048, 3072, 4096
C_GQ, C_GK, C_GV, C_GR = 5120, 5632, 6144, 7168
C_GATE = 8192
N_MAIN = C_GATE + N_BRANCH * D_MODEL

V7X_VMEM_LIMIT = 52 * 1024 * 1024
V7X_VMEM_LIMIT_IN_PROJECTION = 56 * 1024 * 1024

SB_Q_TILE = 512
SB_K_TILE = 256
SB_ROW_CHUNK = 64
SB_HEADS_PER_STEP = 2
GLA_BLOCK = 256
GLA_SUB = 16
DECODE_PAGES_PER_STEP = 16

NT_DIMS = (((1,), (1,)), ((), ()))
TN_DIMS = (((0,), (0,)), ((), ()))


def _params(*sem, vmem=V7X_VMEM_LIMIT):
    return pltpu.CompilerParams(dimension_semantics=sem, vmem_limit_bytes=vmem)


def _dot(a, b):
    return jnp.dot(a, b, preferred_element_type=f32)


def _gelu(x):
    return 0.5 * x * (1.0 + lax.erf(x * 0.7071067811865476))


def _sigmoid(x):
    return 1.0 / (1.0 + jnp.exp(-x))


def _silu(x):
    return x * _sigmoid(x)


def _softplus_neg_abs(z):
    return jnp.log1p(jnp.exp(-jnp.abs(z)))


def _split_bf16(x):
    hi = x.astype(bf16)
    lo = (x - hi.astype(f32)).astype(bf16)
    return hi, lo


def _rms(x, g):
    ms = jnp.mean(x * x, axis=-1, keepdims=True)
    return x * lax.rsqrt(ms + EPS) * g


def _in_projection_kernel(x_ref, g_ref, w_ref, ws_ref, *refs, n_cast):
    src_refs = refs[:n_cast]
    o_ref, os_ref = refs[n_cast:n_cast + 2]
    dst_refs = refs[n_cast + 2:2 * n_cast + 2]
    h_ref = refs[2 * n_cast + 2]

    @pl.when(pl.program_id(1) == 0)
    def _():
        h = _rms(x_ref[...], g_ref[...]).astype(bf16)
        h_ref[...] = h
        os_ref[...] = lax.dot_general(h, ws_ref[...], NT_DIMS, preferred_element_type=f32)

    o_ref[...] = lax.dot_general(h_ref[...], w_ref[...], NT_DIMS, preferred_element_type=f32)
    for src, dst in zip(src_refs, dst_refs):
        dst[...] = src[...].astype(bf16)


def _in_projection_call(x, g, wt, layer, gap_row, gap, tm, tn, casts=()):
    m, d = x.shape
    n_out = wt.shape[1] - gap
    gap_block = gap_row // tn
    ni, nj = m // tm, n_out // tn

    def w_rows(i, j):
        return layer, pl.multiple_of(j * tn + jnp.where(j >= gap_block, gap, 0), BF16_SUBLANES), 0

    cast_in, cast_out, cast_shapes = [], [], []
    for stack, cast_layer in casts:
        rows, cols = stack.shape[1:]
        r = BF16_SUBLANES * pl.cdiv(rows, BF16_SUBLANES * ni * nj)
        last = pl.cdiv(rows, r) - 1

        def block(i, j, cast_layer=cast_layer, last=last):
            return cast_layer, jnp.minimum(i * nj + j, last), 0

        cast_in.append(pl.BlockSpec((None, r, cols), block))
        cast_out.append(pl.BlockSpec((None, r, cols), lambda i, j, last=last: (0, jnp.minimum(i * nj + j, last), 0)))
        cast_shapes.append(jax.ShapeDtypeStruct((1, rows, cols), bf16))

    outs = pl.pallas_call(
        functools.partial(_in_projection_kernel, n_cast=len(casts)),
        out_shape=(jax.ShapeDtypeStruct((m, n_out), f32), jax.ShapeDtypeStruct((m, gap), f32), *cast_shapes),
        grid=(ni, nj),
        in_specs=[pl.BlockSpec((tm, d), lambda i, j: (i, 0)),
                  pl.BlockSpec((1, d), lambda i, j: (0, 0)),
                  pl.BlockSpec((None, pl.Element(tn), pl.Element(d)), w_rows),
                  pl.BlockSpec((None, pl.Element(gap), pl.Element(d)), lambda i, j: (layer, gap_row, 0)),
                  *cast_in],
        out_specs=(pl.BlockSpec((tm, tn), lambda i, j: (i, j)), pl.BlockSpec((tm, gap), lambda i, j: (i, 0)),
                   *cast_out),
        scratch_shapes=[pltpu.VMEM((tm, d), bf16)],
        compiler_params=_params("arbitrary", "arbitrary", vmem=V7X_VMEM_LIMIT_IN_PROJECTION),
        name="in_projection",
    )(x, g, wt, wt, *[stack for stack, _ in casts])
    return outs[0], outs[1], list(outs[2:])


def _sgu_norm(av, gain):
    v = _gelu(av)
    v = v - jnp.mean(v, axis=-1, keepdims=True)
    v = v * lax.rsqrt(jnp.mean(v * v, axis=-1, keepdims=True) + EPS)
    return v * gain


def _sgu_prompt_kernel(au_ref, av_ref, gain_ref, ws_ref, bs_ref, o_ref, sv_ref):
    rows = au_ref.shape[0]
    sv_ref[...] = _sgu_norm(av_ref[...], gain_ref[...]).astype(bf16)
    for c in range(rows // GROUP_W):
        r = pl.ds(c * GROUP_W, GROUP_W)
        for g in range(N_GROUPS):
            cols = pl.ds(g * GROUP_W, GROUP_W)
            s = _dot(ws_ref[g], sv_ref[r, cols]) + bs_ref[:, cols]
            o_ref[r, cols] = (_gelu(au_ref[r, cols]) * s).astype(bf16)


def _sgu_prompt(p, gain, ws_tril, bs_full, rows):
    m = p.shape[0]
    return pl.pallas_call(
        _sgu_prompt_kernel,
        out_shape=jax.ShapeDtypeStruct((m, BRANCH_W), bf16),
        grid=(m // rows,),
        in_specs=[pl.BlockSpec((rows, BRANCH_W), lambda i: (i, C_AU // BRANCH_W)),
                  pl.BlockSpec((rows, BRANCH_W), lambda i: (i, C_AV // BRANCH_W)),
                  pl.BlockSpec((1, BRANCH_W), lambda i: (0, 0)),
                  pl.BlockSpec((N_GROUPS, GROUP_W, GROUP_W), lambda i: (0, 0, 0)),
                  pl.BlockSpec((GROUP_W, BRANCH_W), lambda i: (0, 0))],
        out_specs=pl.BlockSpec((rows, BRANCH_W), lambda i: (i, 0)),
        scratch_shapes=[pltpu.VMEM((rows, BRANCH_W), bf16)],
        compiler_params=_params("parallel"),
        name="sgu_prompt",
    )(p, p, gain, ws_tril, bs_full)


def _sgu_sample_kernel(au_ref, av_ref, gain_ref, w0_ref, b0_ref, o_ref, sv_ref):
    sv = _sgu_norm(av_ref[...], gain_ref[...])
    sv_ref[...] = sv
    o_ref[...] = (_gelu(au_ref[...]) * (w0_ref[...] * sv + b0_ref[...])).astype(bf16)


def _sgu_sample(p, gain, w0, b0):
    m = p.shape[0]
    row = pl.BlockSpec((1, BRANCH_W), lambda i: (0, 0))
    return pl.pallas_call(
        _sgu_sample_kernel,
        out_shape=(jax.ShapeDtypeStruct((m, BRANCH_W), bf16), jax.ShapeDtypeStruct((m, BRANCH_W), f32)),
        grid=(1,),
        in_specs=[pl.BlockSpec((m, BRANCH_W), lambda i: (0, C_AU // BRANCH_W)),
                  pl.BlockSpec((m, BRANCH_W), lambda i: (0, C_AV // BRANCH_W)),
                  row, row, row],
        out_specs=(pl.BlockSpec((m, BRANCH_W), lambda i: (0, 0)), pl.BlockSpec((m, BRANCH_W), lambda i: (0, 0))),
        compiler_params=_params("arbitrary"),
        name="sgu_sample",
    )(p, p, gain, w0, b0)


def _stick_logs(z2, causal=None):
    sp = jnp.log2(1.0 + jnp.exp2(-jnp.abs(z2)))
    log_beta = jnp.minimum(z2, 0.0) - sp
    log_rest = log_beta - z2
    if causal is not None:
        log_rest = jnp.where(causal, log_rest, 0.0)
    return log_beta, log_rest


def _local_tail(log_rest, upper):
    return _dot(log_rest.astype(bf16), upper)


def _strict_upper(t):
    r = lax.broadcasted_iota(jnp.int32, (t, t), 0)
    c = lax.broadcasted_iota(jnp.int32, (t, t), 1)
    return (r > c).astype(bf16)


def _sb_prompt_kernel(bias_ref, q_ref, k_ref, v_ref, o_ref, kb_ref, vb_ref, lb_ref, tail_ref, col_ref, *, tk):
    tq = q_ref.shape[0]
    hp = q_ref.shape[1] // SB_DIM
    g = pl.program_id(0)
    i = pl.program_id(1)

    @pl.when(i == 0)
    def _():
        kb_ref[...] = k_ref[...].astype(bf16)
        vb_ref[...] = v_ref[...].astype(bf16)

    scale = SB_DIM ** -0.5 * LOG2_E
    upper = _strict_upper(tk)
    q_pos = i * tq + lax.broadcasted_iota(jnp.int32, (tq, tk), 0)
    k_off = lax.broadcasted_iota(jnp.int32, (tq, tk), 1)
    heads = [pl.ds(h * SB_DIM, SB_DIM) for h in range(hp)]
    q = [q_ref[:, cols].astype(bf16) for cols in heads]
    bias = [bias_ref[g * hp + h] * LOG2_E for h in range(hp)]

    chunks = [slice(r, r + SB_ROW_CHUNK) for r in range(0, tq, SB_ROW_CHUNK)]

    def key_rows(j):
        return pl.ds(pl.multiple_of(j * tk, tk), tk)

    def logits(j, row0=0):
        return [lax.dot_general(q[h][row0:], kb_ref[key_rows(j), heads[h]], NT_DIMS, preferred_element_type=f32)
                for h in range(hp)]

    def finish_stage_a(z, j, masked, row0=0):
        causal = (j * tk + k_off < q_pos) if masked else None
        rest_bf = []
        for h in range(hp):
            if row0:
                lb_ref[h, :row0, :] = jnp.full((row0, tk), -jnp.inf, f32)
                tail_ref[h, :row0, :] = jnp.zeros((row0, tk), f32)
                col_ref[h, :row0, :] = jnp.zeros((row0, 1), f32)
            parts = []
            for c in chunks:
                if c.start < row0:
                    continue
                zc = z[h][c.start - row0:c.stop - row0]
                log_beta, log_rest = _stick_logs(zc * scale + bias[h], None if causal is None else causal[c])
                if masked:
                    log_beta = jnp.where(causal[c], log_beta, -jnp.inf)
                lb_ref[h, c, :] = log_beta
                col_ref[h, c, :] = log_rest[:, 0:1]
                parts.append(log_rest.astype(bf16))
            rest_bf.append(jnp.concatenate(parts, axis=0))
        for h in range(hp):
            tail_ref[h, row0:, :] = _dot(rest_bf[h], upper)

    def stage_b(state, j):
        out = []
        for h in range(hp):
            carry, acc = state[h]
            a = [jnp.exp2(lb_ref[h, c, :] + tail_ref[h, c, :] + carry[c]).astype(bf16) for c in chunks]
            carry = carry + tail_ref[h, :, 0:1] + col_ref[h]
            out.append((carry, acc + _dot(jnp.concatenate(a, axis=0), vb_ref[key_rows(j), heads[h]])))
        return tuple(out)

    def step(j_next, masked, state, j):
        z = logits(j_next)
        state = stage_b(state, j)
        finish_stage_a(z, j_next, masked)
        return state

    state = tuple((jnp.zeros((tq, 1), f32), jnp.zeros((tq, SB_DIM), f32)) for _ in range(hp))
    n_diag = tq // tk
    first = i * n_diag + n_diag - 1
    finish_stage_a(logits(first, tq - tk), first, True, tq - tk)
    for d in range(1, n_diag):
        state = step(first - d, True, state, first - d + 1)
    state = lax.fori_loop(0, i * n_diag, lambda n, s: step(i * n_diag - 1 - n, False, s, i * n_diag - n), state)
    state = stage_b(state, 0)
    for h in range(hp):
        o_ref[:, heads[h]] = state[h][1].astype(bf16)


def _sb_prompt(p, bias):
    m = p.shape[0]
    tq = min(SB_Q_TILE, m)
    tk = min(SB_K_TILE, m)
    w = SB_HEADS_PER_STEP * SB_DIM
    return pl.pallas_call(
        functools.partial(_sb_prompt_kernel, tk=tk),
        out_shape=jax.ShapeDtypeStruct((m, BRANCH_W), bf16),
        grid=(BRANCH_W // w, m // tq),
        in_specs=[pl.BlockSpec(memory_space=pltpu.SMEM),
                  pl.BlockSpec((tq, w), lambda g, i: (i, C_SQ // w + g)),
                  pl.BlockSpec((m, w), lambda g, i: (0, C_SK // w + g)),
                  pl.BlockSpec((m, w), lambda g, i: (0, C_SV // w + g))],
        out_specs=pl.BlockSpec((tq, w), lambda g, i: (i, g)),
        scratch_shapes=[pltpu.VMEM((m, w), bf16), pltpu.VMEM((m, w), bf16),
                        pltpu.VMEM((SB_HEADS_PER_STEP, tq, tk), f32), pltpu.VMEM((SB_HEADS_PER_STEP, tq, tk), f32),
                        pltpu.VMEM((SB_HEADS_PER_STEP, tq, 1), f32)],
        compiler_params=_params("arbitrary", "arbitrary"),
        name="sb_prompt",
    )(bias, p, p, p)


def _sb_decode_kernel(pt_ref, q_ref, bias_ref, *refs, pps):
    k_refs, v_refs = refs[:pps], refs[pps:2 * pps]
    o_ref, acc_ref, carry_ref, qc_ref = refs[2 * pps:]
    j = pl.program_id(1)

    @pl.when(j == 0)
    def _():
        acc_ref[...] = jnp.zeros_like(acc_ref)
        carry_ref[...] = jnp.zeros_like(carry_ref)
        row = lax.broadcasted_iota(jnp.int32, (PAGE, BRANCH_W), 0)
        lane_head = lax.broadcasted_iota(jnp.int32, (PAGE, BRANCH_W), 1) // SB_DIM
        q_rows = jnp.where(row == lane_head, jnp.broadcast_to(q_ref[...], (PAGE, BRANCH_W)), 0.0)
        qc_ref[...] = q_rows.T.astype(bf16)

    scale = SB_DIM ** -0.5 * LOG2_E
    upper = _strict_upper(PAGE)

    def head_major(refs_):
        return jnp.concatenate(
            [jnp.concatenate([ref[pl.ds(h, PAGE, stride=SB_HEADS), :] for h in range(SB_HEADS)], axis=1)
             for ref in refs_], axis=0).astype(bf16)

    zt = _dot(head_major(k_refs), qc_ref[...])
    z = jnp.concatenate([zt[u * PAGE:(u + 1) * PAGE, :].T[0:SB_HEADS, :] for u in range(pps)], axis=0)
    z = z * scale + bias_ref[...] * LOG2_E
    log_beta, log_rest = _stick_logs(z)
    tail = _local_tail(log_rest, upper)
    totals = tail[:, 0:1] + log_rest[:, 0:1]
    carry = carry_ref[:, 0:1]
    carries = []
    for u in range(pps):
        carries.append(carry)
        carry = carry + totals[u * SB_HEADS:(u + 1) * SB_HEADS, :]
    a = jnp.exp2(log_beta + tail + jnp.concatenate(carries, axis=0))
    a_wide = jnp.concatenate([a[u * SB_HEADS:(u + 1) * SB_HEADS, :] for u in range(pps)], axis=1)
    acc = acc_ref[...] + _dot(a_wide.astype(bf16), head_major(v_refs))
    acc_ref[...] = acc
    carry_ref[...] = jnp.broadcast_to(carry, carry_ref.shape)

    @pl.when(j == pl.num_programs(1) - 1)
    def _():
        sub = lax.broadcasted_iota(jnp.int32, (SB_HEADS, BRANCH_W), 0)
        lane_head = lax.broadcasted_iota(jnp.int32, (SB_HEADS, BRANCH_W), 1) // SB_DIM
        o_ref[...] = jnp.sum(jnp.where(sub == lane_head, acc, 0.0), axis=0, keepdims=True)


def _sb_decode(q, bias_col, cache_k, cache_v, page_table, layer):
    b, n_pages = page_table.shape
    pps = DECODE_PAGES_PER_STEP
    while n_pages % pps:
        pps //= 2
    rows = PAGE * SB_HEADS

    def page_spec(u):
        return pl.BlockSpec((None, None, rows, SB_DIM),
                            lambda s, j, pt: (layer, pt[s, n_pages - 1 - (j * pps + u)], 0, 0))

    grid_spec = pltpu.PrefetchScalarGridSpec(
        num_scalar_prefetch=1,
        grid=(b, n_pages // pps),
        in_specs=[pl.BlockSpec((None, 1, BRANCH_W), lambda s, j, pt: (s, 0, 0)),
                  pl.BlockSpec((pps * SB_HEADS, 1), lambda s, j, pt: (0, 0))]
                 + [page_spec(u) for u in range(pps)] * 2,
        out_specs=pl.BlockSpec((None, 1, BRANCH_W), lambda s, j, pt: (s, 0, 0)),
        scratch_shapes=[pltpu.VMEM((SB_HEADS, BRANCH_W), f32), pltpu.VMEM((SB_HEADS, PAGE), f32),
                        pltpu.VMEM((BRANCH_W, SB_DIM), bf16)],
    )
    return pl.pallas_call(
        functools.partial(_sb_decode_kernel, pps=pps),
        out_shape=jax.ShapeDtypeStruct((b, 1, BRANCH_W), f32),
        grid_spec=grid_spec,
        compiler_params=_params("arbitrary", "arbitrary"),
        name="sb_decode",
    )(page_table, q, jnp.tile(bias_col, (pps, 1)), *([cache_k] * pps), *([cache_v] * pps))


def _gla_log_decay(ga, wa_ref, ba_ref):
    x = _dot(ga.astype(bf16), wa_ref[...]) + ba_ref[...]
    return (jnp.minimum(x, 0.0) - _softplus_neg_abs(x)) * (1.0 / GLA_TAU)


def _gla_out_norm(o, g, r):
    parts = []
    for h in range(GLA_HEADS):
        cols = slice(h * GLA_DV, (h + 1) * GLA_DV)
        parts.append(_rms(o[:, cols], g[:, cols]))
    return jnp.concatenate(parts, axis=1) * _silu(r)


def _gla_prompt_kernel(q_ref, k_ref, v_ref, r_ref, ga_ref, wa_ref, ba_ref, g_ref, o_ref, s_ref,
                       st_ref, b_ref, qe_ref, ke_ref, eb_ref, qs_ref, oacc_ref):
    t = q_ref.shape[0]
    sub = GLA_SUB
    step = pl.program_id(0)

    @pl.when(step == 0)
    def _():
        st_ref[...] = jnp.zeros_like(st_ref)

    log_a = _gla_log_decay(ga_ref[...], wa_ref, ba_ref)
    r = lax.broadcasted_iota(jnp.int32, (t, t), 0)
    c = lax.broadcasted_iota(jnp.int32, (t, t), 1)
    same = (r // sub) == (c // sub)
    incl = (same & (c <= r)).astype(bf16)
    whole = same.astype(bf16)
    hi, lo = _split_bf16(log_a)
    b = (_dot(incl, hi) + _dot(incl, lo)) * LOG2_E
    b_end = (_dot(whole, hi) + _dot(whole, lo)) * LOG2_E
    qs = q_ref[...] * (GLA_DK ** -0.5)
    b_ref[...] = b
    qs_ref[...] = qs
    qe_ref[...] = (qs * jnp.exp2(b)).astype(bf16)
    ke_ref[...] = (k_ref[...] * jnp.exp2(b_end - b)).astype(bf16)
    eb_ref[...] = jnp.exp2(b_end)

    t_col = lax.broadcasted_iota(jnp.int32, (sub, 1), 0)

    def body(n, _):
        rows = pl.ds(pl.multiple_of(n * sub, sub), sub)
        for h in range(GLA_HEADS):
            kc = pl.ds(h * GLA_DK, GLA_DK)
            vc = pl.ds(h * GLA_DV, GLA_DV)
            st = st_ref[h]
            o = lax.dot_general(qe_ref[rows, kc], st.astype(bf16), NT_DIMS, preferred_element_type=f32)
            bi = b_ref[rows, kc]
            qi = qs_ref[rows, kc]
            ki = k_ref[rows, kc]
            vi = v_ref[rows, vc]
            for s in range(sub):
                e = jnp.exp2(jnp.minimum(bi - bi[s:s + 1, :], 0.0))
                sc = jnp.sum(qi * e * ki[s:s + 1, :], axis=-1, keepdims=True)
                o = o + jnp.where(t_col >= s, sc, 0.0) * vi[s:s + 1, :]
            oacc_ref[rows, vc] = o
            upd = lax.dot_general(vi.astype(bf16), ke_ref[rows, kc], TN_DIMS, preferred_element_type=f32)
            st_ref[h] = st * eb_ref[pl.ds(pl.multiple_of(n * sub, sub), 1), kc] + upd
        return 0

    lax.fori_loop(0, t // sub, body, 0)
    o_ref[...] = _gla_out_norm(oacc_ref[...], g_ref[...], r_ref[...]).astype(bf16)

    @pl.when(step == pl.num_programs(0) - 1)
    def _():
        for h in range(GLA_HEADS):
            s_ref[h] = st_ref[h].T


def _gla_prompt(p, pa, wa, ba, g):
    m = p.shape[0]
    t = min(GLA_BLOCK, m)
    return pl.pallas_call(
        _gla_prompt_kernel,
        out_shape=(jax.ShapeDtypeStruct((m, BRANCH_W), bf16),
                   jax.ShapeDtypeStruct((GLA_HEADS, GLA_DK, GLA_DV), f32)),
        grid=(m // t,),
        in_specs=[pl.BlockSpec((t, GLA_KEY), lambda i: (i, C_GQ // GLA_KEY)),
                  pl.BlockSpec((t, GLA_KEY), lambda i: (i, C_GK // GLA_KEY)),
                  pl.BlockSpec((t, BRANCH_W), lambda i: (i, C_GV // BRANCH_W)),
                  pl.BlockSpec((t, BRANCH_W), lambda i: (i, C_GR // BRANCH_W)),
                  pl.BlockSpec((t, GLA_RANK), lambda i: (i, 0)),
                  pl.BlockSpec((GLA_RANK, GLA_KEY), lambda i: (0, 0)),
                  pl.BlockSpec((1, GLA_KEY), lambda i: (0, 0)),
                  pl.BlockSpec((1, BRANCH_W), lambda i: (0, 0))],
        out_specs=(pl.BlockSpec((t, BRANCH_W), lambda i: (i, 0)),
                   pl.BlockSpec((GLA_HEADS, GLA_DK, GLA_DV), lambda i: (0, 0, 0))),
        scratch_shapes=[pltpu.VMEM((GLA_HEADS, GLA_DV, GLA_DK), f32),
                        pltpu.VMEM((t, GLA_KEY), f32),
                        pltpu.VMEM((t, GLA_KEY), bf16),
                        pltpu.VMEM((t, GLA_KEY), bf16),
                        pltpu.VMEM((t, GLA_KEY), f32),
                        pltpu.VMEM((t, GLA_KEY), f32),
                        pltpu.VMEM((t, BRANCH_W), f32)],
        compiler_params=_params("arbitrary"),
        name="gla_prompt",
    )(p, p, p, p, pa, wa, ba, g)


def _gla_sample_kernel(p_ref, ga_ref, s0_ref, wa_ref, ba_ref, g_ref, o_ref, s_ref):
    log_a = _gla_log_decay(ga_ref[...], wa_ref, ba_ref)
    decay = jnp.exp(log_a)
    q = p_ref[:, C_GQ:C_GQ + GLA_KEY] * (GLA_DK ** -0.5)
    k = p_ref[:, C_GK:C_GK + GLA_KEY]
    v = p_ref[:, C_GV:C_GV + BRANCH_W]
    row = lax.broadcasted_iota(jnp.int32, (8, GLA_DK), 0)
    parts = []
    for h in range(GLA_HEADS):
        kc = slice(h * GLA_DK, (h + 1) * GLA_DK)
        vc = slice(h * GLA_DV, (h + 1) * GLA_DV)
        s0 = s0_ref[h]
        score = jnp.sum(q[:, kc] * k[:, kc], axis=-1, keepdims=True)
        o = score * v[:, vc] + _dot((q[:, kc] * decay[:, kc]).astype(bf16), s0.astype(bf16))
        parts.append(o)
        tile = jnp.where(row == 0, k[:, kc], jnp.where(row == 1, decay[:, kc], 0.0))
        cols = tile.T
        s_ref[h] = cols[:, 1:2] * s0 + cols[:, 0:1] * v[:, vc]
    o_ref[...] = _gla_out_norm(jnp.concatenate(parts, axis=1), g_ref[...],
                               p_ref[:, C_GR:C_GR + BRANCH_W]).astype(bf16)


def _gla_sample(p3, pa3, state, layer, wa, ba, g):
    b = p3.shape[0]
    return pl.pallas_call(
        _gla_sample_kernel,
        out_shape=(jax.ShapeDtypeStruct((b, 1, BRANCH_W), bf16),
                   jax.ShapeDtypeStruct((b, GLA_HEADS, GLA_DK, GLA_DV), f32)),
        grid=(b,),
        in_specs=[pl.BlockSpec((None, 1, N_MAIN), lambda i: (i, 0, 0)),
                  pl.BlockSpec((None, 1, GLA_RANK), lambda i: (i, 0, 0)),
                  pl.BlockSpec((None, None, GLA_HEADS, GLA_DK, GLA_DV), lambda i: (layer, i, 0, 0, 0)),
                  pl.BlockSpec((GLA_RANK, GLA_KEY), lambda i: (0, 0)),
                  pl.BlockSpec((1, GLA_KEY), lambda i: (0, 0)),
                  pl.BlockSpec((1, BRANCH_W), lambda i: (0, 0))],
        out_specs=(pl.BlockSpec((None, 1, BRANCH_W), lambda i: (i, 0, 0)),
                   pl.BlockSpec((None, GLA_HEADS, GLA_DK, GLA_DV), lambda i: (i, 0, 0, 0))),
        compiler_params=_params("parallel"),
        name="gla_sample",
    )(p3, pa3, state, wa, ba, g)


def _merge_kernel(oa_ref, ob_ref, oc_ref, w_ref, ga_ref, gb_ref, gc_ref, o_ref):
    y = _sigmoid(ga_ref[...]) * _dot(oa_ref[...], w_ref[0])
    y = y + _sigmoid(gb_ref[...]) * _dot(ob_ref[...], w_ref[1])
    y = y + _sigmoid(gc_ref[...]) * _dot(oc_ref[...], w_ref[2])
    o_ref[...] = y.astype(bf16)


def _merge(oa, ob, oc, w_branch, layer, p, tm, tn):
    m = oa.shape[0]
    nj = D_MODEL // tn
    branch = pl.BlockSpec((tm, BRANCH_W), lambda i, j: (i, 0))

    def gate(n):
        return pl.BlockSpec((tm, tn), lambda i, j: (i, (C_GATE + n * D_MODEL) // tn + j))

    return pl.pallas_call(
        _merge_kernel,
        out_shape=jax.ShapeDtypeStruct((m, D_MODEL), bf16),
        grid=(m // tm, nj),
        in_specs=[branch, branch, branch,
                  pl.BlockSpec((None, N_BRANCH, BRANCH_W, tn), lambda i, j: (layer, 0, 0, j)),
                  gate(0), gate(1), gate(2)],
        out_specs=pl.BlockSpec((tm, tn), lambda i, j: (i, j)),
        compiler_params=_params("parallel", "arbitrary"),
        name="merge",
    )(oa, ob, oc, w_branch, p, p, p)


def _matmul_residual_kernel(a_ref, w_ref, x_ref, o_ref):
    o_ref[...] = x_ref[...] + _dot(a_ref[...], w_ref[...])


def _matmul_residual(a, w, layer, x, tm, tn):
    m, k = a.shape
    n = w.shape[2]
    return pl.pallas_call(
        _matmul_residual_kernel,
        out_shape=jax.ShapeDtypeStruct((m, n), f32),
        grid=(m // tm, n // tn),
        in_specs=[pl.BlockSpec((tm, k), lambda i, j: (i, 0)),
                  pl.BlockSpec((None, k, tn), lambda i, j: (layer, 0, j)),
                  pl.BlockSpec((tm, tn), lambda i, j: (i, j))],
        out_specs=pl.BlockSpec((tm, tn), lambda i, j: (i, j)),
        compiler_params=_params("parallel", "arbitrary"),
        name="matmul_residual",
    )(a, w, x)


def _ffn_in_kernel(x_ref, g_ref, wg_ref, wu_ref, o_ref, h_ref):
    @pl.when(pl.program_id(1) == 0)
    def _():
        h_ref[...] = _rms(x_ref[...], g_ref[...]).astype(bf16)

    h = h_ref[...]
    o_ref[...] = (_silu(_dot(h, wg_ref[...])) * _dot(h, wu_ref[...])).astype(bf16)


def _ffn_in(x, g, w, layer, tm, tn):
    m, d = x.shape
    nj = D_FF // tn
    return pl.pallas_call(
        _ffn_in_kernel,
        out_shape=jax.ShapeDtypeStruct((m, D_FF), bf16),
        grid=(m // tm, nj),
        in_specs=[pl.BlockSpec((tm, d), lambda i, j: (i, 0)),
                  pl.BlockSpec((1, d), lambda i, j: (0, 0)),
                  pl.BlockSpec((None, d, tn), lambda i, j: (layer, 0, j)),
                  pl.BlockSpec((None, d, tn), lambda i, j: (layer, 0, nj + j))],
        out_specs=pl.BlockSpec((tm, tn), lambda i, j: (i, j)),
        scratch_shapes=[pltpu.VMEM((tm, d), bf16)],
        compiler_params=_params("parallel", "arbitrary"),
        name="ffn_in",
    )(x, g, w, w)


def _kv_rows_kernel(*refs, depth):
    p_refs, (ko_ref, vo_ref) = refs[:2 * depth], refs[2 * depth:]
    layer = pl.program_id(0)
    rows = p_refs[0].shape[0]
    for l in range(depth):
        @pl.when(layer == l)
        def _():
            for src, dst in ((p_refs[2 * l], ko_ref), (p_refs[2 * l + 1], vo_ref)):
                for h in range(SB_HEADS):
                    dst[pl.ds(h, rows, stride=SB_HEADS), :] = src[:, h * SB_DIM:(h + 1) * SB_DIM]


def _kv_rows(ps, tm):
    depth = len(ps)
    m = ps[0].shape[0]
    nb = m // tm

    def src(l, col):
        return pl.BlockSpec((tm, BRANCH_W), lambda d, i: (jnp.where(d == l, i, jnp.where(d < l, 0, nb - 1)),
                                                          col // BRANCH_W))

    out = jax.ShapeDtypeStruct((depth, m * SB_HEADS, SB_DIM), f32)
    dst = pl.BlockSpec((None, tm * SB_HEADS, SB_DIM), lambda d, i: (d, i, 0))
    k, v = pl.pallas_call(
        functools.partial(_kv_rows_kernel, depth=depth),
        out_shape=(out, out),
        grid=(depth, nb),
        in_specs=[src(l, col) for l in range(depth) for col in (C_SK, C_SV)],
        out_specs=(dst, dst),
        compiler_params=_params("arbitrary", "arbitrary"),
        name="kv_rows",
    )(*[p for p in ps for _ in range(2)])
    shape = (depth, 1, m, SB_HEADS, SB_DIM)
    return k.reshape(shape), v.reshape(shape)


def _final_norm_kernel(x_ref, g_ref, o_ref):
    o_ref[...] = _rms(x_ref[...], g_ref[...])


def _final_norm(x, g, tm):
    m, d = x.shape
    return pl.pallas_call(
        _final_norm_kernel,
        out_shape=jax.ShapeDtypeStruct((m, d), f32),
        grid=(m // tm,),
        in_specs=[pl.BlockSpec((tm, d), lambda i: (i, 0)), pl.BlockSpec((1, d), lambda i: (0, 0))],
        out_specs=pl.BlockSpec((tm, d), lambda i: (i, 0)),
        compiler_params=_params("parallel"),
        name="final_norm",
    )(x, g)


CAST_ORDER = ("w_branch", "w_o", "w_ffn_in", "w_ffn_out")


def _weight_stacks(w_in, w_branch, w_o, w_ffn_in, w_ffn_out):
    depth = w_in.shape[0]
    return dict(
        w_in_t=jnp.swapaxes(w_in, 1, 2),
        w_branch=w_branch.reshape(depth, N_BRANCH * BRANCH_W, D_MODEL),
        w_o=w_o,
        w_ffn_in=w_ffn_in,
        w_ffn_out=w_ffn_out,
    )


def _layer_params(l, g_mix, sgu_gain, w_spatial, b_spatial, sb_bias, w_gla_a2, b_gla_a, g_gla_out, g_ffn):
    return dict(
        g_mix=g_mix[l][None, :],
        sgu_gain=sgu_gain[l][None, :],
        ws_tril=jnp.tril(w_spatial[l]).astype(bf16),
        bs_full=jnp.repeat(b_spatial[l].T, GROUP_W, axis=1),
        ws_first=jnp.repeat(w_spatial[l][:, 0, 0], GROUP_W)[None, :],
        bs_first=jnp.repeat(b_spatial[l][:, 0], GROUP_W)[None, :],
        sb_bias=sb_bias[l],
        sb_bias_col=sb_bias[l][:, None],
        w_a2=w_gla_a2[l].astype(bf16),
        b_a=b_gla_a[l][None, :],
        g_gla=g_gla_out[l][None, :],
        g_ffn=g_ffn[l][None, :],
    )


def _tile(m, want):
    return want if m % want == 0 else m


def _in_projection(x, w, w_in_bf, casts=()):
    tm = _tile(x.shape[0], 1024)
    tn = 1024 if tm == 1024 else 2048
    return _in_projection_call(x, w["g_mix"], w_in_bf, 0, C_GATE, GLA_RANK, tm, tn, casts)


def _channel_mix(x, oa, ob, oc, p, w, mw):
    tm = _tile(x.shape[0], 1024)
    t_merge, t_o, t_in, t_out = (512, 1024, 512, 512) if tm == 1024 else (1024, 2048, D_FF // 4, 1024)
    merged = _merge(oa, ob, oc, mw["w_branch"].reshape(1, N_BRANCH, BRANCH_W, D_MODEL), 0, p, tm, t_merge)
    x = _matmul_residual(merged, mw["w_o"], 0, x, tm, t_o)
    f = _ffn_in(x, w["g_ffn"], mw["w_ffn_in"], 0, tm, t_in)
    return _matmul_residual(f, mw["w_ffn_out"], 0, x, tm, t_out)


def kernel(x_prompt, x_sample, cache_k, cache_v, state_gla, page_table, g_mix, w_in, sgu_gain, w_spatial, b_spatial, sb_bias, w_gla_a2, b_gla_a, g_gla_out, w_branch, w_o, g_ffn, w_ffn_in, w_ffn_out, g_final):
    depth = w_in.shape[0]
    batch, seq, _ = x_prompt.shape
    dec_batch, dec_seq, _ = x_sample.shape
    assert batch == 1 and dec_seq == 1
    assert cache_k.shape[2:] == (PAGE, SB_HEADS, SB_DIM)
    n_phys = cache_k.shape[1]
    ck = cache_k.reshape(depth, n_phys, PAGE * SB_HEADS, SB_DIM)
    cv = cache_v.reshape(depth, n_phys, PAGE * SB_HEADS, SB_DIM)

    xp = x_prompt.reshape(seq, D_MODEL)
    xs = x_sample.reshape(dec_batch, D_MODEL)
    tms = dec_batch
    stacks = _weight_stacks(w_in, w_branch, w_o, w_ffn_in, w_ffn_out)
    w_in_bf = stacks["w_in_t"][0:1].astype(bf16)
    p_p, s_p, p_s, s_s, cv_s = [], [], [], [], []
    for l in range(depth):
        w = _layer_params(l, g_mix, sgu_gain, w_spatial, b_spatial, sb_bias, w_gla_a2, b_gla_a, g_gla_out, g_ffn)
        casts = [(stacks[name], l) for name in CAST_ORDER] + ([(stacks["w_in_t"], l + 1)] if l + 1 < depth else [])
        p, pa, cast = _in_projection(xp, w, w_in_bf, casts)
        mw = dict(zip(CAST_ORDER, cast))
        oa = _sgu_prompt(p, w["sgu_gain"], w["ws_tril"], w["bs_full"], _tile(seq, 256))
        ob = _sb_prompt(p, w["sb_bias"])
        oc, s_new = _gla_prompt(p, pa, w["w_a2"], w["b_a"], w["g_gla"])
        xp = _channel_mix(xp, oa, ob, oc, p, w, mw)
        p_p.append(p)
        s_p.append(s_new[None])
        p, pa, _ = _in_projection(xs, w, w_in_bf)
        oa, sgu_v = _sgu_sample(p, w["sgu_gain"], w["ws_first"], w["bs_first"])
        q = p[:, C_SQ:C_SQ + BRANCH_W].reshape(dec_batch, 1, BRANCH_W)
        ob = _sb_decode(q, w["sb_bias_col"], ck, cv, page_table, l).reshape(dec_batch, BRANCH_W).astype(bf16)
        oc, s_new = _gla_sample(p.reshape(dec_batch, 1, N_MAIN), pa.reshape(dec_batch, 1, GLA_RANK), state_gla, l,
                                w["w_a2"], w["b_a"], w["g_gla"])
        xs = _channel_mix(xs, oa, ob, oc.reshape(dec_batch, BRANCH_W), p, w, mw)
        p_s.append(p)
        s_s.append(s_new)
        cv_s.append(sgu_v.reshape(dec_batch, 1, BRANCH_W))
        if l + 1 < depth:
            w_in_bf = cast[len(CAST_ORDER)]

    g_fin = g_final[None, :]
    y_prompt = _final_norm(xp, g_fin, _tile(seq, 512)).reshape(1, seq, D_MODEL)
    y_sample = _final_norm(xs, g_fin, tms).reshape(dec_batch, 1, D_MODEL)
    k_prompt, v_prompt = _kv_rows(p_p, _tile(seq, 512))
    k_sample, v_sample = (a.reshape(depth, dec_batch, 1, SB_HEADS, SB_DIM) for a in _kv_rows(p_s, dec_batch))
    return (y_prompt, y_sample, k_prompt, v_prompt, k_sample, v_sample,
            jnp.stack(s_p), jnp.stack(s_s), jnp.stack(cv_s))
```

```python
import functools

import jax
import jax.numpy as jnp
from jax import lax
from jax.experimental import pallas as pl
from jax.experimental.pallas import tpu as pltpu

f32 = jnp.float32
bf16 = jnp.bfloat16

D_MODEL = 2048
BRANCH_W = 1024
GROUP_W = 128
N_GROUPS = 8
SB_HEADS = 8
SB_DIM = 128
GLA_HEADS = 4
GLA_DK = 128
GLA_DV = 256
GLA_KEY = GLA_HEADS * GLA_DK
GLA_RANK = 16
GLA_TAU = 16.0
N_BRANCH = 3
D_FF = 5632
EPS = 1e-6
PAGE = 128
BF16_SUBLANES = 16
LOG2_E = 1.4426950408889634

C_AU, C_AV, C_SQ, C_SK, C_SV = 0, 1024, 2048, 3072, 4096
C_GQ, C_GK, C_GV, C_GR = 5120, 5632, 6144, 7168
C_GATE = 8192
N_MAIN = C_GATE + N_BRANCH * D_MODEL

V7X_VMEM_LIMIT = 52 * 1024 * 1024
V7X_VMEM_LIMIT_IN_PROJECTION = 56 * 1024 * 1024

SB_Q_TILE = 512
SB_K_TILE = 256
SB_ROW_CHUNK = 64
SB_HEADS_PER_STEP = 2
GLA_BLOCK = 256
GLA_SUB = 16
DECODE_PAGES_PER_STEP = 16

NT_DIMS = (((1,), (1,)), ((), ()))
TN_DIMS = (((0,), (0,)), ((), ()))


def _params(*sem, vmem=V7X_VMEM_LIMIT):
    return pltpu.CompilerParams(dimension_semantics=sem, vmem_limit_bytes=vmem)


def _dot(a, b):
    return jnp.dot(a, b, preferred_element_type=f32)


def _gelu(x):
    return 0.5 * x * (1.0 + lax.erf(x * 0.7071067811865476))


def _sigmoid(x):
    return 1.0 / (1.0 + jnp.exp(-x))


def _silu(x):
    return x * _sigmoid(x)


def _softplus_neg_abs(z):
    return jnp.log1p(jnp.exp(-jnp.abs(z)))


def _split_bf16(x):
    hi = x.astype(bf16)
    lo = (x - hi.astype(f32)).astype(bf16)
    return hi, lo


def _rms(x, g):
    ms = jnp.mean(x * x, axis=-1, keepdims=True)
    return x * lax.rsqrt(ms + EPS) * g


def _in_projection_kernel(x_ref, g_ref, w_ref, ws_ref, *refs, n_cast):
    src_refs = refs[:n_cast]
    o_ref, os_ref = refs[n_cast:n_cast + 2]
    dst_refs = refs[n_cast + 2:2 * n_cast + 2]
    h_ref = refs[2 * n_cast + 2]

    @pl.when(pl.program_id(1) == 0)
    def _():
        h = _rms(x_ref[...], g_ref[...]).astype(bf16)
        h_ref[...] = h
        os_ref[...] = lax.dot_general(h, ws_ref[...], NT_DIMS, preferred_element_type=f32)

    o_ref[...] = lax.dot_general(h_ref[...], w_ref[...], NT_DIMS, preferred_element_type=f32)
    for src, dst in zip(src_refs, dst_refs):
        dst[...] = src[...].astype(bf16)


def _in_projection_call(x, g, wt, layer, gap_row, gap, tm, tn, casts=()):
    m, d = x.shape
    n_out = wt.shape[1] - gap
    gap_block = gap_row // tn
    ni, nj = m // tm, n_out // tn

    def w_rows(i, j):
        return layer, pl.multiple_of(j * tn + jnp.where(j >= gap_block, gap, 0), BF16_SUBLANES), 0

    cast_in, cast_out, cast_shapes = [], [], []
    for stack, cast_layer in casts:
        rows, cols = stack.shape[1:]
        r = BF16_SUBLANES * pl.cdiv(rows, BF16_SUBLANES * ni * nj)
        last = pl.cdiv(rows, r) - 1

        def block(i, j, cast_layer=cast_layer, last=last):
            return cast_layer, jnp.minimum(i * nj + j, last), 0

        cast_in.append(pl.BlockSpec((None, r, cols), block))
        cast_out.append(pl.BlockSpec((None, r, cols), lambda i, j, last=last: (0, jnp.minimum(i * nj + j, last), 0)))
        cast_shapes.append(jax.ShapeDtypeStruct((1, rows, cols), bf16))

    outs = pl.pallas_call(
        functools.partial(_in_projection_kernel, n_cast=len(casts)),
        out_shape=(jax.ShapeDtypeStruct((m, n_out), f32), jax.ShapeDtypeStruct((m, gap), f32), *cast_shapes),
        grid=(ni, nj),
        in_specs=[pl.BlockSpec((tm, d), lambda i, j: (i, 0)),
                  pl.BlockSpec((1, d), lambda i, j: (0, 0)),
                  pl.BlockSpec((None, pl.Element(tn), pl.Element(d)), w_rows),
                  pl.BlockSpec((None, pl.Element(gap), pl.Element(d)), lambda i, j: (layer, gap_row, 0)),
                  *cast_in],
        out_specs=(pl.BlockSpec((tm, tn), lambda i, j: (i, j)), pl.BlockSpec((tm, gap), lambda i, j: (i, 0)),
                   *cast_out),
        scratch_shapes=[pltpu.VMEM((tm, d), bf16)],
        compiler_params=_params("arbitrary", "arbitrary", vmem=V7X_VMEM_LIMIT_IN_PROJECTION),
        name="in_projection",
    )(x, g, wt, wt, *[stack for stack, _ in casts])
    return outs[0], outs[1], list(outs[2:])


def _sgu_norm(av, gain):
    v = _gelu(av)
    v = v - jnp.mean(v, axis=-1, keepdims=True)
    v = v * lax.rsqrt(jnp.mean(v * v, axis=-1, keepdims=True) + EPS)
    return v * gain


def _sgu_prompt_kernel(au_ref, av_ref, gain_ref, ws_ref, bs_ref, o_ref, sv_ref):
    rows = au_ref.shape[0]
    sv_ref[...] = _sgu_norm(av_ref[...], gain_ref[...]).astype(bf16)
    for c in range(rows // GROUP_W):
        r = pl.ds(c * GROUP_W, GROUP_W)
        for g in range(N_GROUPS):
            cols = pl.ds(g * GROUP_W, GROUP_W)
            s = _dot(ws_ref[g], sv_ref[r, cols]) + bs_ref[:, cols]
            o_ref[r, cols] = (_gelu(au_ref[r, cols]) * s).astype(bf16)


def _sgu_prompt(p, gain, ws_tril, bs_full, rows):
    m = p.shape[0]
    return pl.pallas_call(
        _sgu_prompt_kernel,
        out_shape=jax.ShapeDtypeStruct((m, BRANCH_W), bf16),
        grid=(m // rows,),
        in_specs=[pl.BlockSpec((rows, BRANCH_W), lambda i: (i, C_AU // BRANCH_W)),
                  pl.BlockSpec((rows, BRANCH_W), lambda i: (i, C_AV // BRANCH_W)),
                  pl.BlockSpec((1, BRANCH_W), lambda i: (0, 0)),
                  pl.BlockSpec((N_GROUPS, GROUP_W, GROUP_W), lambda i: (0, 0, 0)),
                  pl.BlockSpec((GROUP_W, BRANCH_W), lambda i: (0, 0))],
        out_specs=pl.BlockSpec((rows, BRANCH_W), lambda i: (i, 0)),
        scratch_shapes=[pltpu.VMEM((rows, BRANCH_W), bf16)],
        compiler_params=_params("parallel"),
        name="sgu_prompt",
    )(p, p, gain, ws_tril, bs_full)


def _sgu_sample_kernel(au_ref, av_ref, gain_ref, w0_ref, b0_ref, o_ref, sv_ref):
    sv = _sgu_norm(av_ref[...], gain_ref[...])
    sv_ref[...] = sv
    o_ref[...] = (_gelu(au_ref[...]) * (w0_ref[...] * sv + b0_ref[...])).astype(bf16)


def _sgu_sample(p, gain, w0, b0):
    m = p.shape[0]
    row = pl.BlockSpec((1, BRANCH_W), lambda i: (0, 0))
    return pl.pallas_call(
        _sgu_sample_kernel,
        out_shape=(jax.ShapeDtypeStruct((m, BRANCH_W), bf16), jax.ShapeDtypeStruct((m, BRANCH_W), f32)),
        grid=(1,),
        in_specs=[pl.BlockSpec((m, BRANCH_W), lambda i: (0, C_AU // BRANCH_W)),
                  pl.BlockSpec((m, BRANCH_W), lambda i: (0, C_AV // BRANCH_W)),
                  row, row, row],
        out_specs=(pl.BlockSpec((m, BRANCH_W), lambda i: (0, 0)), pl.BlockSpec((m, BRANCH_W), lambda i: (0, 0))),
        compiler_params=_params("arbitrary"),
        name="sgu_sample",
    )(p, p, gain, w0, b0)


def _stick_logs(z2, causal=None):
    sp = jnp.log2(1.0 + jnp.exp2(-jnp.abs(z2)))
    log_beta = jnp.minimum(z2, 0.0) - sp
    log_rest = log_beta - z2
    if causal is not None:
        log_rest = jnp.where(causal, log_rest, 0.0)
    return log_beta, log_rest


def _local_tail(log_rest, upper):
    return _dot(log_rest.astype(bf16), upper)


def _strict_upper(t):
    r = lax.broadcasted_iota(jnp.int32, (t, t), 0)
    c = lax.broadcasted_iota(jnp.int32, (t, t), 1)
    return (r > c).astype(bf16)


def _sb_prompt_kernel(bias_ref, q_ref, k_ref, v_ref, o_ref, kb_ref, vb_ref, lb_ref, tail_ref, col_ref, *, tk):
    tq = q_ref.shape[0]
    hp = q_ref.shape[1] // SB_DIM
    g = pl.program_id(0)
    i = pl.program_id(1)

    @pl.when(i == 0)
    def _():
        kb_ref[...] = k_ref[...].astype(bf16)
        vb_ref[...] = v_ref[...].astype(bf16)

    scale = SB_DIM ** -0.5 * LOG2_E
    upper = _strict_upper(tk)
    q_pos = i * tq + lax.broadcasted_iota(jnp.int32, (tq, tk), 0)
    k_off = lax.broadcasted_iota(jnp.int32, (tq, tk), 1)
    heads = [pl.ds(h * SB_DIM, SB_DIM) for h in range(hp)]
    q = [(q_ref[:, cols] * scale).astype(bf16) for cols in heads]
    bias = [bias_ref[g * hp + h] * LOG2_E for h in range(hp)]

    chunks = [slice(r, r + SB_ROW_CHUNK) for r in range(0, tq, SB_ROW_CHUNK)]

    def key_rows(j):
        return pl.ds(pl.multiple_of(j * tk, tk), tk)

    def logits(j, row0=0):
        return [lax.dot_general(q[h][row0:], kb_ref[key_rows(j), heads[h]], NT_DIMS, preferred_element_type=f32)
                for h in range(hp)]

    def finish_stage_a(z, j, masked, row0=0):
        causal = (j * tk + k_off < q_pos) if masked else None
        rest_bf = []
        for h in range(hp):
            if row0:
                lb_ref[h, :row0, :] = jnp.full((row0, tk), -jnp.inf, f32)
                tail_ref[h, :row0, :] = jnp.zeros((row0, tk), f32)
                col_ref[h, :row0, :] = jnp.zeros((row0, 1), f32)
            parts = []
            for c in chunks:
                if c.start < row0:
                    continue
                zc = z[h][c.start - row0:c.stop - row0]
                log_beta, log_rest = _stick_logs(zc + bias[h], None if causal is None else causal[c])
                if masked:
                    log_beta = jnp.where(causal[c], log_beta, -jnp.inf)
                lb_ref[h, c, :] = log_beta
                col_ref[h, c, :] = log_rest[:, 0:1]
                parts.append(log_rest.astype(bf16))
            rest_bf.append(jnp.concatenate(parts, axis=0))
        for h in range(hp):
            tail_ref[h, row0:, :] = _dot(rest_bf[h], upper)

    def stage_b(state, j):
        out = []
        for h in range(hp):
            carry, acc = state[h]
            a = [jnp.exp2(lb_ref[h, c, :] + tail_ref[h, c, :] + carry[c]).astype(bf16) for c in chunks]
            carry = carry + tail_ref[h, :, 0:1] + col_ref[h]
            out.append((carry, acc + _dot(jnp.concatenate(a, axis=0), vb_ref[key_rows(j), heads[h]])))
        return tuple(out)

    def step(j_next, masked, state, j):
        z = logits(j_next)
        state = stage_b(state, j)
        finish_stage_a(z, j_next, masked)
        return state

    state = tuple((jnp.zeros((tq, 1), f32), jnp.zeros((tq, SB_DIM), f32)) for _ in range(hp))
    n_diag = tq // tk
    first = i * n_diag + n_diag - 1
    finish_stage_a(logits(first, tq - tk), first, True, tq - tk)
    for d in range(1, n_diag):
        state = step(first - d, True, state, first - d + 1)
    state = lax.fori_loop(0, i * n_diag, lambda n, s: step(i * n_diag - 1 - n, False, s, i * n_diag - n), state)
    state = stage_b(state, 0)
    for h in range(hp):
        o_ref[:, heads[h]] = state[h][1].astype(bf16)


def _sb_prompt(p, bias):
    m = p.shape[0]
    tq = min(SB_Q_TILE, m)
    tk = min(SB_K_TILE, m)
    w = SB_HEADS_PER_STEP * SB_DIM
    return pl.pallas_call(
        functools.partial(_sb_prompt_kernel, tk=tk),
        out_shape=jax.ShapeDtypeStruct((m, BRANCH_W), bf16),
        grid=(BRANCH_W // w, m // tq),
        in_specs=[pl.BlockSpec(memory_space=pltpu.SMEM),
                  pl.BlockSpec((tq, w), lambda g, i: (i, C_SQ // w + g)),
                  pl.BlockSpec((m, w), lambda g, i: (0, C_SK // w + g)),
                  pl.BlockSpec((m, w), lambda g, i: (0, C_SV // w + g))],
        out_specs=pl.BlockSpec((tq, w), lambda g, i: (i, g)),
        scratch_shapes=[pltpu.VMEM((m, w), bf16), pltpu.VMEM((m, w), bf16),
                        pltpu.VMEM((SB_HEADS_PER_STEP, tq, tk), f32), pltpu.VMEM((SB_HEADS_PER_STEP, tq, tk), f32),
                        pltpu.VMEM((SB_HEADS_PER_STEP, tq, 1), f32)],
        compiler_params=_params("arbitrary", "arbitrary"),
        name="sb_prompt",
    )(bias, p, p, p)


def _sb_decode_kernel(pt_ref, q_ref, bias_ref, *refs, pps):
    k_refs, v_refs = refs[:pps], refs[pps:2 * pps]
    o_ref, acc_ref, carry_ref, qc_ref = refs[2 * pps:]
    j = pl.program_id(1)

    @pl.when(j == 0)
    def _():
        acc_ref[...] = jnp.zeros_like(acc_ref)
        carry_ref[...] = jnp.zeros_like(carry_ref)
        row = lax.broadcasted_iota(jnp.int32, (PAGE, BRANCH_W), 0)
        lane_head = lax.broadcasted_iota(jnp.int32, (PAGE, BRANCH_W), 1) // SB_DIM
        q_rows = jnp.where(row == lane_head, jnp.broadcast_to(q_ref[...], (PAGE, BRANCH_W)), 0.0)
        qc_ref[...] = q_rows.T.astype(bf16)

    scale = SB_DIM ** -0.5 * LOG2_E
    upper = _strict_upper(PAGE)

    def head_major(refs_):
        return jnp.concatenate(
            [jnp.concatenate([ref[pl.ds(h, PAGE, stride=SB_HEADS), :] for h in range(SB_HEADS)], axis=1)
             for ref in refs_], axis=0).astype(bf16)

    zt = _dot(head_major(k_refs), qc_ref[...])
    z = jnp.concatenate([zt[u * PAGE:(u + 1) * PAGE, :].T[0:SB_HEADS, :] for u in range(pps)], axis=0)
    z = z * scale + bias_ref[...] * LOG2_E
    log_beta, log_rest = _stick_logs(z)
    tail = _local_tail(log_rest, upper)
    totals = tail[:, 0:1] + log_rest[:, 0:1]
    carry = carry_ref[:, 0:1]
    carries = []
    for u in range(pps):
        carries.append(carry)
        carry = carry + totals[u * SB_HEADS:(u + 1) * SB_HEADS, :]
    a = jnp.exp2(log_beta + tail + jnp.concatenate(carries, axis=0))
    a_wide = jnp.concatenate([a[u * SB_HEADS:(u + 1) * SB_HEADS, :] for u in range(pps)], axis=1)
    acc = acc_ref[...] + _dot(a_wide.astype(bf16), head_major(v_refs))
    acc_ref[...] = acc
    carry_ref[...] = jnp.broadcast_to(carry, carry_ref.shape)

    @pl.when(j == pl.num_programs(1) - 1)
    def _():
        sub = lax.broadcasted_iota(jnp.int32, (SB_HEADS, BRANCH_W), 0)
        lane_head = lax.broadcasted_iota(jnp.int32, (SB_HEADS, BRANCH_W), 1) // SB_DIM
        o_ref[...] = jnp.sum(jnp.where(sub == lane_head, acc, 0.0), axis=0, keepdims=True)


def _sb_decode(q, bias_col, cache_k, cache_v, page_table, layer):
    b, n_pages = page_table.shape
    pps = DECODE_PAGES_PER_STEP
    while n_pages % pps:
        pps //= 2
    rows = PAGE * SB_HEADS

    def page_spec(u):
        return pl.BlockSpec((None, None, rows, SB_DIM),
                            lambda s, j, pt: (layer, pt[s, n_pages - 1 - (j * pps + u)], 0, 0))

    grid_spec = pltpu.PrefetchScalarGridSpec(
        num_scalar_prefetch=1,
        grid=(b, n_pages // pps),
        in_specs=[pl.BlockSpec((None, 1, BRANCH_W), lambda s, j, pt: (s, 0, 0)),
                  pl.BlockSpec((pps * SB_HEADS, 1), lambda s, j, pt: (0, 0))]
                 + [page_spec(u) for u in range(pps)] * 2,
        out_specs=pl.BlockSpec((None, 1, BRANCH_W), lambda s, j, pt: (s, 0, 0)),
        scratch_shapes=[pltpu.VMEM((SB_HEADS, BRANCH_W), f32), pltpu.VMEM((SB_HEADS, PAGE), f32),
                        pltpu.VMEM((BRANCH_W, SB_DIM), bf16)],
    )
    return pl.pallas_call(
        functools.partial(_sb_decode_kernel, pps=pps),
        out_shape=jax.ShapeDtypeStruct((b, 1, BRANCH_W), f32),
        grid_spec=grid_spec,
        compiler_params=_params("arbitrary", "arbitrary"),
        name="sb_decode",
    )(page_table, q, jnp.tile(bias_col, (pps, 1)), *([cache_k] * pps), *([cache_v] * pps))


def _gla_log_decay(ga, wa_ref, ba_ref):
    x = _dot(ga.astype(bf16), wa_ref[...]) + ba_ref[...]
    return (jnp.minimum(x, 0.0) - _softplus_neg_abs(x)) * (1.0 / GLA_TAU)


def _gla_out_norm(o, g, r):
    parts = []
    for h in range(GLA_HEADS):
        cols = slice(h * GLA_DV, (h + 1) * GLA_DV)
        parts.append(_rms(o[:, cols], g[:, cols]))
    return jnp.concatenate(parts, axis=1) * _silu(r)


def _gla_prompt_kernel(q_ref, k_ref, v_ref, r_ref, ga_ref, wa_ref, ba_ref, g_ref, o_ref, s_ref,
                       st_ref, b_ref, qe_ref, ke_ref, eb_ref, qs_ref, oacc_ref):
    t = q_ref.shape[0]
    sub = GLA_SUB
    step = pl.program_id(0)

    @pl.when(step == 0)
    def _():
        st_ref[...] = jnp.zeros_like(st_ref)

    log_a = _gla_log_decay(ga_ref[...], wa_ref, ba_ref)
    r = lax.broadcasted_iota(jnp.int32, (t, t), 0)
    c = lax.broadcasted_iota(jnp.int32, (t, t), 1)
    same = (r // sub) == (c // sub)
    incl = (same & (c <= r)).astype(bf16)
    whole = same.astype(bf16)
    hi, lo = _split_bf16(log_a)
    b = (_dot(incl, hi) + _dot(incl, lo)) * LOG2_E
    b_end = (_dot(whole, hi) + _dot(whole, lo)) * LOG2_E
    qs = q_ref[...] * (GLA_DK ** -0.5)
    b_ref[...] = b
    qs_ref[...] = qs
    qe_ref[...] = (qs * jnp.exp2(b)).astype(bf16)
    ke_ref[...] = (k_ref[...] * jnp.exp2(b_end - b)).astype(bf16)
    eb_ref[...] = jnp.exp2(b_end)

    t_col = lax.broadcasted_iota(jnp.int32, (sub, 1), 0)

    def body(n, _):
        rows = pl.ds(pl.multiple_of(n * sub, sub), sub)
        for h in range(GLA_HEADS):
            kc = pl.ds(h * GLA_DK, GLA_DK)
            vc = pl.ds(h * GLA_DV, GLA_DV)
            st = st_ref[h]
            o = lax.dot_general(qe_ref[rows, kc], st.astype(bf16), NT_DIMS, preferred_element_type=f32)
            bi = b_ref[rows, kc]
            qi = qs_ref[rows, kc]
            ki = k_ref[rows, kc]
            vi = v_ref[rows, vc]
            for s in range(sub):
                e = jnp.exp2(jnp.minimum(bi - bi[s:s + 1, :], 0.0))
                sc = jnp.sum(qi * e * ki[s:s + 1, :], axis=-1, keepdims=True)
                o = o + jnp.where(t_col >= s, sc, 0.0) * vi[s:s + 1, :]
            oacc_ref[rows, vc] = o
            upd = lax.dot_general(vi.astype(bf16), ke_ref[rows, kc], TN_DIMS, preferred_element_type=f32)
            st_ref[h] = st * eb_ref[pl.ds(pl.multiple_of(n * sub, sub), 1), kc] + upd
        return 0

    lax.fori_loop(0, t // sub, body, 0)
    o_ref[...] = _gla_out_norm(oacc_ref[...], g_ref[...], r_ref[...]).astype(bf16)

    @pl.when(step == pl.num_programs(0) - 1)
    def _():
        for h in range(GLA_HEADS):
            s_ref[h] = st_ref[h].T


def _gla_prompt(p, pa, wa, ba, g):
    m = p.shape[0]
    t = min(GLA_BLOCK, m)
    return pl.pallas_call(
        _gla_prompt_kernel,
        out_shape=(jax.ShapeDtypeStruct((m, BRANCH_W), bf16),
                   jax.ShapeDtypeStruct((GLA_HEADS, GLA_DK, GLA_DV), f32)),
        grid=(m // t,),
        in_specs=[pl.BlockSpec((t, GLA_KEY), lambda i: (i, C_GQ // GLA_KEY)),
                  pl.BlockSpec((t, GLA_KEY), lambda i: (i, C_GK // GLA_KEY)),
                  pl.BlockSpec((t, BRANCH_W), lambda i: (i, C_GV // BRANCH_W)),
                  pl.BlockSpec((t, BRANCH_W), lambda i: (i, C_GR // BRANCH_W)),
                  pl.BlockSpec((t, GLA_RANK), lambda i: (i, 0)),
                  pl.BlockSpec((GLA_RANK, GLA_KEY), lambda i: (0, 0)),
                  pl.BlockSpec((1, GLA_KEY), lambda i: (0, 0)),
                  pl.BlockSpec((1, BRANCH_W), lambda i: (0, 0))],
        out_specs=(pl.BlockSpec((t, BRANCH_W), lambda i: (i, 0)),
                   pl.BlockSpec((GLA_HEADS, GLA_DK, GLA_DV), lambda i: (0, 0, 0))),
        scratch_shapes=[pltpu.VMEM((GLA_HEADS, GLA_DV, GLA_DK), f32),
                        pltpu.VMEM((t, GLA_KEY), f32),
                        pltpu.VMEM((t, GLA_KEY), bf16),
                        pltpu.VMEM((t, GLA_KEY), bf16),
                        pltpu.VMEM((t, GLA_KEY), f32),
                        pltpu.VMEM((t, GLA_KEY), f32),
                        pltpu.VMEM((t, BRANCH_W), f32)],
        compiler_params=_params("arbitrary"),
        name="gla_prompt",
    )(p, p, p, p, pa, wa, ba, g)


def _gla_sample_kernel(p_ref, ga_ref, s0_ref, wa_ref, ba_ref, g_ref, o_ref, s_ref):
    log_a = _gla_log_decay(ga_ref[...], wa_ref, ba_ref)
    decay = jnp.exp(log_a)
    q = p_ref[:, C_GQ:C_GQ + GLA_KEY] * (GLA_DK ** -0.5)
    k = p_ref[:, C_GK:C_GK + GLA_KEY]
    v = p_ref[:, C_GV:C_GV + BRANCH_W]
    row = lax.broadcasted_iota(jnp.int32, (8, GLA_DK), 0)
    parts = []
    for h in range(GLA_HEADS):
        kc = slice(h * GLA_DK, (h + 1) * GLA_DK)
        vc = slice(h * GLA_DV, (h + 1) * GLA_DV)
        s0 = s0_ref[h]
        score = jnp.sum(q[:, kc] * k[:, kc], axis=-1, keepdims=True)
        o = score * v[:, vc] + _dot((q[:, kc] * decay[:, kc]).astype(bf16), s0.astype(bf16))
        parts.append(o)
        tile = jnp.where(row == 0, k[:, kc], jnp.where(row == 1, decay[:, kc], 0.0))
        cols = tile.T
        s_ref[h] = cols[:, 1:2] * s0 + cols[:, 0:1] * v[:, vc]
    o_ref[...] = _gla_out_norm(jnp.concatenate(parts, axis=1), g_ref[...],
                               p_ref[:, C_GR:C_GR + BRANCH_W]).astype(bf16)


def _gla_sample(p3, pa3, state, layer, wa, ba, g):
    b = p3.shape[0]
    return pl.pallas_call(
        _gla_sample_kernel,
        out_shape=(jax.ShapeDtypeStruct((b, 1, BRANCH_W), bf16),
                   jax.ShapeDtypeStruct((b, GLA_HEADS, GLA_DK, GLA_DV), f32)),
        grid=(b,),
        in_specs=[pl.BlockSpec((None, 1, N_MAIN), lambda i: (i, 0, 0)),
                  pl.BlockSpec((None, 1, GLA_RANK), lambda i: (i, 0, 0)),
                  pl.BlockSpec((None, None, GLA_HEADS, GLA_DK, GLA_DV), lambda i: (layer, i, 0, 0, 0)),
                  pl.BlockSpec((GLA_RANK, GLA_KEY), lambda i: (0, 0)),
                  pl.BlockSpec((1, GLA_KEY), lambda i: (0, 0)),
                  pl.BlockSpec((1, BRANCH_W), lambda i: (0, 0))],
        out_specs=(pl.BlockSpec((None, 1, BRANCH_W), lambda i: (i, 0, 0)),
                   pl.BlockSpec((None, GLA_HEADS, GLA_DK, GLA_DV), lambda i: (i, 0, 0, 0))),
        compiler_params=_params("parallel"),
        name="gla_sample",
    )(p3, pa3, state, wa, ba, g)


def _merge_kernel(oa_ref, ob_ref, oc_ref, w_ref, ga_ref, gb_ref, gc_ref, o_ref):
    y = _sigmoid(ga_ref[...]) * _dot(oa_ref[...], w_ref[0])
    y = y + _sigmoid(gb_ref[...]) * _dot(ob_ref[...], w_ref[1])
    y = y + _sigmoid(gc_ref[...]) * _dot(oc_ref[...], w_ref[2])
    o_ref[...] = y.astype(bf16)


def _merge(oa, ob, oc, w_branch, layer, p, tm, tn):
    m = oa.shape[0]
    nj = D_MODEL // tn
    branch = pl.BlockSpec((tm, BRANCH_W), lambda i, j: (i, 0))

    def gate(n):
        return pl.BlockSpec((tm, tn), lambda i, j: (i, (C_GATE + n * D_MODEL) // tn + j))

    return pl.pallas_call(
        _merge_kernel,
        out_shape=jax.ShapeDtypeStruct((m, D_MODEL), bf16),
        grid=(m // tm, nj),
        in_specs=[branch, branch, branch,
                  pl.BlockSpec((None, N_BRANCH, BRANCH_W, tn), lambda i, j: (layer, 0, 0, j)),
                  gate(0), gate(1), gate(2)],
        out_specs=pl.BlockSpec((tm, tn), lambda i, j: (i, j)),
        compiler_params=_params("parallel", "arbitrary"),
        name="merge",
    )(oa, ob, oc, w_branch, p, p, p)


def _matmul_residual_kernel(a_ref, w_ref, x_ref, o_ref):
    o_ref[...] = x_ref[...] + _dot(a_ref[...], w_ref[...])


def _matmul_residual(a, w, layer, x, tm, tn):
    m, k = a.shape
    n = w.shape[2]
    return pl.pallas_call(
        _matmul_residual_kernel,
        out_shape=jax.ShapeDtypeStruct((m, n), f32),
        grid=(m // tm, n // tn),
        in_specs=[pl.BlockSpec((tm, k), lambda i, j: (i, 0)),
                  pl.BlockSpec((None, k, tn), lambda i, j: (layer, 0, j)),
                  pl.BlockSpec((tm, tn), lambda i, j: (i, j))],
        out_specs=pl.BlockSpec((tm, tn), lambda i, j: (i, j)),
        compiler_params=_params("parallel", "arbitrary"),
        name="matmul_residual",
    )(a, w, x)


def _ffn_in_kernel(x_ref, g_ref, wg_ref, wu_ref, o_ref, h_ref):
    @pl.when(pl.program_id(1) == 0)
    def _():
        h_ref[...] = _rms(x_ref[...], g_ref[...]).astype(bf16)

    h = h_ref[...]
    o_ref[...] = (_silu(_dot(h, wg_ref[...])) * _dot(h, wu_ref[...])).astype(bf16)


def _ffn_in(x, g, w, layer, tm, tn):
    m, d = x.shape
    nj = D_FF // tn
    return pl.pallas_call(
        _ffn_in_kernel,
        out_shape=jax.ShapeDtypeStruct((m, D_FF), bf16),
        grid=(m // tm, nj),
        in_specs=[pl.BlockSpec((tm, d), lambda i, j: (i, 0)),
                  pl.BlockSpec((1, d), lambda i, j: (0, 0)),
                  pl.BlockSpec((None, d, tn), lambda i, j: (layer, 0, j)),
                  pl.BlockSpec((None, d, tn), lambda i, j: (layer, 0, nj + j))],
        out_specs=pl.BlockSpec((tm, tn), lambda i, j: (i, j)),
        scratch_shapes=[pltpu.VMEM((tm, d), bf16)],
        compiler_params=_params("parallel", "arbitrary"),
        name="ffn_in",
    )(x, g, w, w)


def _kv_rows_kernel(*refs, depth):
    p_refs, (ko_ref, vo_ref) = refs[:2 * depth], refs[2 * depth:]
    layer = pl.program_id(0)
    rows = p_refs[0].shape[0]
    for l in range(depth):
        @pl.when(layer == l)
        def _():
            for src, dst in ((p_refs[2 * l], ko_ref), (p_refs[2 * l + 1], vo_ref)):
                for h in range(SB_HEADS):
                    dst[pl.ds(h, rows, stride=SB_HEADS), :] = src[:, h * SB_DIM:(h + 1) * SB_DIM]


def _kv_rows(ps, tm):
    depth = len(ps)
    m = ps[0].shape[0]
    nb = m // tm

    def src(l, col):
        return pl.BlockSpec((tm, BRANCH_W), lambda d, i: (jnp.where(d == l, i, jnp.where(d < l, 0, nb - 1)),
                                                          col // BRANCH_W))

    out = jax.ShapeDtypeStruct((depth, m * SB_HEADS, SB_DIM), f32)
    dst = pl.BlockSpec((None, tm * SB_HEADS, SB_DIM), lambda d, i: (d, i, 0))
    k, v = pl.pallas_call(
        functools.partial(_kv_rows_kernel, depth=depth),
        out_shape=(out, out),
        grid=(depth, nb),
        in_specs=[src(l, col) for l in range(depth) for col in (C_SK, C_SV)],
        out_specs=(dst, dst),
        compiler_params=_params("arbitrary", "arbitrary"),
        name="kv_rows",
    )(*[p for p in ps for _ in range(2)])
    shape = (depth, 1, m, SB_HEADS, SB_DIM)
    return k.reshape(shape), v.reshape(shape)


def _final_norm_kernel(x_ref, g_ref, o_ref):
    o_ref[...] = _rms(x_ref[...], g_ref[...])


def _final_norm(x, g, tm):
    m, d = x.shape
    return pl.pallas_call(
        _final_norm_kernel,
        out_shape=jax.ShapeDtypeStruct((m, d), f32),
        grid=(m // tm,),
        in_specs=[pl.BlockSpec((tm, d), lambda i: (i, 0)), pl.BlockSpec((1, d), lambda i: (0, 0))],
        out_specs=pl.BlockSpec((tm, d), lambda i: (i, 0)),
        compiler_params=_params("parallel"),
        name="final_norm",
    )(x, g)


CAST_ORDER = ("w_branch", "w_o", "w_ffn_in", "w_ffn_out")


def _weight_stacks(w_in, w_branch, w_o, w_ffn_in, w_ffn_out):
    depth = w_in.shape[0]
    return dict(
        w_in_t=jnp.swapaxes(w_in, 1, 2),
        w_branch=w_branch.reshape(depth, N_BRANCH * BRANCH_W, D_MODEL),
        w_o=w_o,
        w_ffn_in=w_ffn_in,
        w_ffn_out=w_ffn_out,
    )


def _layer_params(l, g_mix, sgu_gain, w_spatial, b_spatial, sb_bias, w_gla_a2, b_gla_a, g_gla_out, g_ffn):
    return dict(
        g_mix=g_mix[l][None, :],
        sgu_gain=sgu_gain[l][None, :],
        ws_tril=jnp.tril(w_spatial[l]).astype(bf16),
        bs_full=jnp.repeat(b_spatial[l].T, GROUP_W, axis=1),
        ws_first=jnp.repeat(w_spatial[l][:, 0, 0], GROUP_W)[None, :],
        bs_first=jnp.repeat(b_spatial[l][:, 0], GROUP_W)[None, :],
        sb_bias=sb_bias[l],
        sb_bias_col=sb_bias[l][:, None],
        w_a2=w_gla_a2[l].astype(bf16),
        b_a=b_gla_a[l][None, :],
        g_gla=g_gla_out[l][None, :],
        g_ffn=g_ffn[l][None, :],
    )


def _tile(m, want):
    return want if m % want == 0 else m


def _in_projection(x, w, w_in_bf, casts=()):
    tm = _tile(x.shape[0], 1024)
    tn = 1024 if tm == 1024 else 2048
    return _in_projection_call(x, w["g_mix"], w_in_bf, 0, C_GATE, GLA_RANK, tm, tn, casts)


def _channel_mix(x, oa, ob, oc, p, w, mw):
    tm = _tile(x.shape[0], 1024)
    t_merge, t_o, t_in, t_out = (512, 1024, 512, 512) if tm == 1024 else (1024, 2048, D_FF // 4, 1024)
    merged = _merge(oa, ob, oc, mw["w_branch"].reshape(1, N_BRANCH, BRANCH_W, D_MODEL), 0, p, tm, t_merge)
    x = _matmul_residual(merged, mw["w_o"], 0, x, tm, t_o)
    f = _ffn_in(x, w["g_ffn"], mw["w_ffn_in"], 0, tm, t_in)
    return _matmul_residual(f, mw["w_ffn_out"], 0, x, tm, t_out)


def kernel(x_prompt, x_sample, cache_k, cache_v, state_gla, page_table, g_mix, w_in, sgu_gain, w_spatial, b_spatial, sb_bias, w_gla_a2, b_gla_a, g_gla_out, w_branch, w_o, g_ffn, w_ffn_in, w_ffn_out, g_final):
    depth = w_in.shape[0]
    batch, seq, _ = x_prompt.shape
    dec_batch, dec_seq, _ = x_sample.shape
    assert batch == 1 and dec_seq == 1
    assert cache_k.shape[2:] == (PAGE, SB_HEADS, SB_DIM)
    n_phys = cache_k.shape[1]
    ck = cache_k.reshape(depth, n_phys, PAGE * SB_HEADS, SB_DIM)
    cv = cache_v.reshape(depth, n_phys, PAGE * SB_HEADS, SB_DIM)

    xp = x_prompt.reshape(seq, D_MODEL)
    xs = x_sample.reshape(dec_batch, D_MODEL)
    tms = dec_batch
    stacks = _weight_stacks(w_in, w_branch, w_o, w_ffn_in, w_ffn_out)
    w_in_bf = stacks["w_in_t"][0:1].astype(bf16)
    p_p, s_p, p_s, s_s, cv_s = [], [], [], [], []
    for l in range(depth):
        w = _layer_params(l, g_mix, sgu_gain, w_spatial, b_spatial, sb_bias, w_gla_a2, b_gla_a, g_gla_out, g_ffn)
        casts = [(stacks[name], l) for name in CAST_ORDER] + ([(stacks["w_in_t"], l + 1)] if l + 1 < depth else [])
        p, pa, cast = _in_projection(xp, w, w_in_bf, casts)
        mw = dict(zip(CAST_ORDER, cast))
        oa = _sgu_prompt(p, w["sgu_gain"], w["ws_tril"], w["bs_full"], _tile(seq, 256))
        ob = _sb_prompt(p, w["sb_bias"])
        oc, s_new = _gla_prompt(p, pa, w["w_a2"], w["b_a"], w["g_gla"])
        xp = _channel_mix(xp, oa, ob, oc, p, w, mw)
        p_p.append(p)
        s_p.append(s_new[None])
        p, pa, _ = _in_projection(xs, w, w_in_bf)
        oa, sgu_v = _sgu_sample(p, w["sgu_gain"], w["ws_first"], w["bs_first"])
        q = p[:, C_SQ:C_SQ + BRANCH_W].reshape(dec_batch, 1, BRANCH_W)
        ob = _sb_decode(q, w["sb_bias_col"], ck, cv, page_table, l).reshape(dec_batch, BRANCH_W).astype(bf16)
        oc, s_new = _gla_sample(p.reshape(dec_batch, 1, N_MAIN), pa.reshape(dec_batch, 1, GLA_RANK), state_gla, l,
                                w["w_a2"], w["b_a"], w["g_gla"])
        xs = _channel_mix(xs, oa, ob, oc.reshape(dec_batch, BRANCH_W), p, w, mw)
        p_s.append(p)
        s_s.append(s_new)
        cv_s.append(sgu_v.reshape(dec_batch, 1, BRANCH_W))
        if l + 1 < depth:
            w_in_bf = cast[len(CAST_ORDER)]

    g_fin = g_final[None, :]
    y_prompt = _final_norm(xp, g_fin, _tile(seq, 512)).reshape(1, seq, D_MODEL)
    y_sample = _final_norm(xs, g_fin, tms).reshape(dec_batch, 1, D_MODEL)
    k_prompt, v_prompt = _kv_rows(p_p, _tile(seq, 512))
    k_sample, v_sample = (a.reshape(depth, dec_batch, 1, SB_HEADS, SB_DIM) for a in _kv_rows(p_s, dec_batch))
    return (y_prompt, y_sample, k_prompt, v_prompt, k_sample, v_sample,
            jnp.stack(s_p), jnp.stack(s_s), jnp.stack(cv_s))
```
